```python
import math
import jax, jax.numpy as jnp
from jax import lax
import numpy as np

D_MODEL = 1024
BATCH = 16
SEQ = 4096
DEPTH = 1

N_MEM = 256
D_MIX = D_MODEL
D_MLSTM = D_MIX // 2
N_MLSTM_HEADS = 4
MLSTM_HEAD_DIM = D_MLSTM // N_MLSTM_HEADS
MLSTM_CHUNK = 128
CONV_WIDTH = 5
D_GMLP = D_MIX - D_MLSTM
N_GMLP_GROUPS = 4
GMLP_GROUP_DIM = D_GMLP // N_GMLP_GROUPS
GMLP_CHUNK = 128
N_GATES = 4 * N_MLSTM_HEADS
PROJ_COLS = 4 * D_MLSTM + N_GATES + 2 * D_GMLP
N_XATTN_HEADS = 4
XATTN_HEAD_DIM = D_MODEL // N_XATTN_HEADS
N_EXPERT_GROUPS = 4
EXPERTS_PER_GROUP = 8
N_EXPERTS = N_EXPERT_GROUPS * EXPERTS_PER_GROUP
TOP_K_IN_GROUP = 2
D_EXPERT = D_MODEL // 2
MOE_BLOCK = 128
RMS_EPS = 1e-6
LN_EPS = 1e-5
NEG_INIT = -1e30

kernel_name = 'hybrid_mlstm_gmlp_xattn_hmoe_encoder'


def rms_norm(x, g):
    xf = x.astype(jnp.float32)
    y = xf * lax.rsqrt(jnp.mean(xf * xf, axis=-1, keepdims=True) + RMS_EPS)
    return (y * g.astype(jnp.float32)).astype(x.dtype)


def layer_norm(x, g, b):
    xf = x.astype(jnp.float32)
    mu = jnp.mean(xf, axis=-1, keepdims=True)
    xc = xf - mu
    y = xc * lax.rsqrt(jnp.mean(xc * xc, axis=-1, keepdims=True) + LN_EPS)
    return (y * g.astype(jnp.float32) + b.astype(jnp.float32)).astype(x.dtype)


def mlstm_chunkwise(q, k, v, i_pre, f_pre):
    Bq, H, S, dh = q.shape
    L = MLSTM_CHUNK
    NC = S // L
    q = q.reshape(Bq, H, NC, L, dh)
    k = k.reshape(Bq, H, NC, L, dh) * (dh ** -0.5)
    v = v.reshape(Bq, H, NC, L, dh)
    ig = i_pre.reshape(Bq, H, NC, L)
    b = jnp.cumsum(jax.nn.log_sigmoid(f_pre.reshape(Bq, H, NC, L)), axis=-1)
    F = b[..., -1]
    a = F[..., None] - b + ig
    m_loc = jnp.max(a, axis=-1)
    kw = k * jnp.exp(a - m_loc[..., None])[..., None]
    C_loc = jnp.einsum('bhcsd,bhcse->bhcde', kw, v)
    n_loc = jnp.sum(kw, axis=-2)

    def step(carry, inp):
        C, n, m = carry
        F_c, m_c, C_c, n_c = inp
        m_new = jnp.maximum(F_c + m, m_c)
        s_old = jnp.exp(F_c + m - m_new)
        s_new = jnp.exp(m_c - m_new)
        C_new = s_old[..., None, None] * C + s_new[..., None, None] * C_c
        n_new = s_old[..., None] * n + s_new[..., None] * n_c
        return (C_new, n_new, m_new), (C, n, m)

    init = (jnp.zeros((Bq, H, dh, dh), jnp.float32),
            jnp.zeros((Bq, H, dh), jnp.float32),
            jnp.full((Bq, H), NEG_INIT, jnp.float32))
    xs = (jnp.moveaxis(F, 2, 0), jnp.moveaxis(m_loc, 2, 0),
          jnp.moveaxis(C_loc, 2, 0), jnp.moveaxis(n_loc, 2, 0))
    _, (C_prev, n_prev, m_prev) = lax.scan(step, init, xs)
    C_prev = jnp.moveaxis(C_prev, 0, 2)
    n_prev = jnp.moveaxis(n_prev, 0, 2)
    m_prev = jnp.moveaxis(m_prev, 0, 2)

    mask = jnp.tril(jnp.ones((L, L), dtype=bool))
    Dm = jnp.where(mask, b[..., :, None] - b[..., None, :] + ig[..., None, :], -jnp.inf)
    inter = b + m_prev[..., None]
    m_t = jnp.maximum(inter, jnp.max(Dm, axis=-1))
    Wqk = jnp.exp(Dm - m_t[..., None]) * jnp.einsum('bhctd,bhcsd->bhcts', q, k)
    s_inter = jnp.exp(inter - m_t)
    num = (jnp.einsum('bhcts,bhcse->bhcte', Wqk, v)
           + s_inter[..., None] * jnp.einsum('bhctd,bhcde->bhcte', q, C_prev))
    den = jnp.sum(Wqk, axis=-1) + s_inter * jnp.einsum('bhctd,bhcd->bhct', q, n_prev)
    h = num / jnp.maximum(jnp.abs(den), jnp.exp(-m_t))[..., None]
    return h.reshape(Bq, H, S, dh)


def parallel_mixer(h, w_in, conv_w, conv_b, gate_b, g_head, ln_v_g, ln_v_b, w_s, b_s, w_out):
    Bq, S, _ = h.shape
    z = h @ w_in
    cuts = [D_MLSTM, 2 * D_MLSTM, 3 * D_MLSTM, 4 * D_MLSTM,
            4 * D_MLSTM + N_GATES, 4 * D_MLSTM + N_GATES + D_GMLP]
    q, k, v, o, gates, gu, gv = jnp.split(z, cuts, axis=-1)

    qk = lax.conv_general_dilated(jnp.concatenate([q, k], axis=-1), conv_w, (1,), 'SAME',
                                  dimension_numbers=('NWC', 'WIO', 'NWC'),
                                  feature_group_count=2 * D_MLSTM)
    qk = jax.nn.silu(qk + conv_b)
    q, k = qk[..., :D_MLSTM], qk[..., D_MLSTM:]

    def heads(t):
        return t.reshape(Bq, S, N_MLSTM_HEADS, MLSTM_HEAD_DIM).transpose(0, 2, 1, 3).astype(jnp.float32)

    qh, kh, vh = heads(q), heads(k), heads(v)
    g = (gates.astype(jnp.float32) + gate_b.astype(jnp.float32)).reshape(Bq, S, 4, N_MLSTM_HEADS)
    g = g.transpose(2, 0, 3, 1)
    i_f, f_f, i_b, f_b = g[0], g[1], g[2], g[3]

    def flip(t):
        return jnp.flip(t, axis=2)

    h_fwd = mlstm_chunkwise(qh, kh, vh, i_f, f_f)
    h_bwd = flip(mlstm_chunkwise(flip(qh), flip(kh), flip(vh), flip(i_b), flip(f_b)))
    hs = h_fwd + h_bwd
    gh = g_head.astype(jnp.float32).reshape(N_MLSTM_HEADS, 1, MLSTM_HEAD_DIM)
    hs = hs * lax.rsqrt(jnp.mean(hs * hs, axis=-1, keepdims=True) + RMS_EPS) * gh
    hs = hs.transpose(0, 2, 1, 3).reshape(Bq, S, D_MLSTM)
    y_mlstm = (jax.nn.sigmoid(o.astype(jnp.float32)) * hs).astype(h.dtype)

    gu = jax.nn.gelu(gu)
    gv = layer_norm(jax.nn.gelu(gv), ln_v_g, ln_v_b)
    NC = S // GMLP_CHUNK
    gv = gv.reshape(Bq, NC, GMLP_CHUNK, N_GMLP_GROUPS, GMLP_GROUP_DIM)
    sp = jnp.einsum('gts,bcsgd->bctgd', w_s, gv) + b_s.T[:, :, None]
    y_gmlp = gu * sp.reshape(Bq, S, D_GMLP)

    return jnp.concatenate([y_mlstm, y_gmlp], axis=-1) @ w_out


def memory_cross_attention(h, hm, w_q, w_kv, w_o):
    Bq, S, _ = h.shape
    M = hm.shape[1]
    q = (h @ w_q).reshape(Bq, S, N_XATTN_HEADS, XATTN_HEAD_DIM)
    kv = (hm @ w_kv).reshape(Bq, M, 2, N_XATTN_HEADS, XATTN_HEAD_DIM)
    k, v = kv[:, :, 0], kv[:, :, 1]
    s = jnp.einsum('bshd,bmhd->bhsm', q, k).astype(jnp.float32) * (XATTN_HEAD_DIM ** -0.5)
    p = jax.nn.softmax(s, axis=-1).astype(h.dtype)
    o = jnp.einsum('bhsm,bmhd->bshd', p, v).reshape(Bq, S, N_XATTN_HEADS * XATTN_HEAD_DIM)
    return o @ w_o


def hier_moe(h, w_rg, b_rg, w_re, b_re, w_gate, w_up, w_down):
    T, D = h.shape
    g_logits = (h @ w_rg).astype(jnp.float32) + b_rg.astype(jnp.float32)
    p_group = jax.nn.softmax(g_logits, axis=-1)
    p_top, g_idx = lax.top_k(p_group, 1)
    e_logits = ((h @ w_re).astype(jnp.float32) + b_re.astype(jnp.float32)).reshape(
        T, N_EXPERT_GROUPS, EXPERTS_PER_GROUP)
    e_logits = jnp.take_along_axis(e_logits, g_idx[:, :, None], axis=1)[:, 0]
    e_top, e_local = lax.top_k(e_logits, TOP_K_IN_GROUP)
    weights = p_top * jax.nn.softmax(e_top, axis=-1)
    expert_id = g_idx * EXPERTS_PER_GROUP + e_local

    A = T * TOP_K_IN_GROUP
    e_flat = expert_id.reshape(A)
    t_flat = jnp.repeat(jnp.arange(T), TOP_K_IN_GROUP)
    w_flat = weights.reshape(A)
    order = jnp.argsort(e_flat)
    e_sorted, t_sorted, w_sorted = e_flat[order], t_flat[order], w_flat[order]
    counts = jnp.bincount(e_flat, length=N_EXPERTS)
    padded = ((counts + MOE_BLOCK - 1) // MOE_BLOCK) * MOE_BLOCK
    start = jnp.cumsum(counts) - counts
    pend = jnp.cumsum(padded)
    pstart = pend - padded
    dest = pstart[e_sorted] + (jnp.arange(A) - start[e_sorted])
    NB = A // MOE_BLOCK + N_EXPERTS
    P = NB * MOE_BLOCK
    x_disp = jnp.zeros((P, D), h.dtype).at[dest].set(h[t_sorted])
    blk_e = jnp.clip(jnp.searchsorted(pend, jnp.arange(NB) * MOE_BLOCK, side='right'), 0, N_EXPERTS - 1)

    def expert_block(args):
        xb, e = args
        return (jax.nn.silu(xb @ w_gate[e]) * (xb @ w_up[e])) @ w_down[e]

    y_disp = lax.map(expert_block, (x_disp.reshape(NB, MOE_BLOCK, D), blk_e)).reshape(P, D)
    contrib = (w_sorted[:, None] * y_disp[dest].astype(jnp.float32))
    out = jnp.zeros((T, D), jnp.float32).at[t_sorted].add(contrib)
    return out.astype(h.dtype)


def setup_inputs(seed: int = 0) -> dict:
    key = jax.random.key(seed)
    ks = jax.random.split(key, 32)

    def nrm(k, shape, scale):
        return jax.random.normal(k, shape, jnp.float32) * scale

    def gain(k, shape):
        return 1.0 + 0.1 * jax.random.normal(k, shape, jnp.float32)

    H = N_MLSTM_HEADS
    f_bias = jnp.linspace(3.0, 6.0, H, dtype=jnp.float32)
    gk = jax.random.split(ks[6], 4)
    gate_b = jnp.concatenate([
        nrm(gk[0], (DEPTH, H), 0.1),
        f_bias + nrm(gk[1], (DEPTH, H), 0.1),
        nrm(gk[2], (DEPTH, H), 0.1),
        f_bias + nrm(gk[3], (DEPTH, H), 0.1)], axis=-1)
    return {
        'x': nrm(ks[0], (BATCH, SEQ, D_MODEL), 1.0),
        'mem': nrm(ks[1], (BATCH, N_MEM, D_MODEL), 1.0),
        'g_mix': gain(ks[2], (DEPTH, D_MODEL)),
        'w_in': nrm(ks[3], (DEPTH, D_MODEL, PROJ_COLS), D_MODEL ** -0.5),
        'conv_w': nrm(ks[4], (DEPTH, CONV_WIDTH, 1, 2 * D_MLSTM), CONV_WIDTH ** -0.5),
        'conv_b': nrm(ks[5], (DEPTH, 2 * D_MLSTM), 0.01),
        'gate_b': gate_b,
        'g_head': gain(ks[7], (DEPTH, D_MLSTM)),
        'ln_v_g': gain(ks[8], (DEPTH, D_GMLP)),
        'ln_v_b': nrm(ks[9], (DEPTH, D_GMLP), 0.01),
        'w_s': nrm(ks[10], (DEPTH, N_GMLP_GROUPS, GMLP_CHUNK, GMLP_CHUNK), GMLP_CHUNK ** -0.5),
        'b_s': gain(ks[11], (DEPTH, N_GMLP_GROUPS, GMLP_CHUNK)),
        'w_out': nrm(ks[12], (DEPTH, D_MIX, D_MODEL), D_MIX ** -0.5),
        'g_xattn': gain(ks[13], (DEPTH, D_MODEL)),
        'g_mem': gain(ks[14], (DEPTH, D_MODEL)),
        'w_q_x': nrm(ks[15], (DEPTH, D_MODEL, N_XATTN_HEADS * XATTN_HEAD_DIM), D_MODEL ** -0.5),
        'w_kv_x': nrm(ks[16], (DEPTH, D_MODEL, 2 * N_XATTN_HEADS * XATTN_HEAD_DIM), D_MODEL ** -0.5),
        'w_o_x': nrm(ks[17], (DEPTH, N_XATTN_HEADS * XATTN_HEAD_DIM, D_MODEL), D_MODEL ** -0.5),
        'g_moe': gain(ks[18], (DEPTH, D_MODEL)),
        'w_rg': nrm(ks[19], (DEPTH, D_MODEL, N_EXPERT_GROUPS), D_MODEL ** -0.5),
        'b_rg': nrm(ks[20], (DEPTH, N_EXPERT_GROUPS), 0.01),
        'w_re': nrm(ks[21], (DEPTH, D_MODEL, N_EXPERTS), D_MODEL ** -0.5),
        'b_re': nrm(ks[22], (DEPTH, N_EXPERTS), 0.01),
        'w_gate': nrm(ks[23], (DEPTH, N_EXPERTS, D_MODEL, D_EXPERT), D_MODEL ** -0.5),
        'w_up': nrm(ks[24], (DEPTH, N_EXPERTS, D_MODEL, D_EXPERT), D_MODEL ** -0.5),
        'w_down': nrm(ks[25], (DEPTH, N_EXPERTS, D_EXPERT, D_MODEL), D_EXPERT ** -0.5),
        'g_final': gain(ks[26], (D_MODEL,)),
    }


def reference(x, mem, g_mix, w_in, conv_w, conv_b, gate_b, g_head, ln_v_g, ln_v_b, w_s, b_s, w_out,
              g_xattn, g_mem, w_q_x, w_kv_x, w_o_x, g_moe, w_rg, b_rg, w_re, b_re,
              w_gate, w_up, w_down, g_final):
    Bq, S, D = x.shape
    for l in range(DEPTH):
        x = x + parallel_mixer(rms_norm(x, g_mix[l]), w_in[l], conv_w[l], conv_b[l], gate_b[l],
                               g_head[l], ln_v_g[l], ln_v_b[l], w_s[l], b_s[l], w_out[l])
        x = x + memory_cross_attention(rms_norm(x, g_xattn[l]), rms_norm(mem, g_mem[l]),
                                       w_q_x[l], w_kv_x[l], w_o_x[l])
        x = x + hier_moe(rms_norm(x, g_moe[l]).reshape(Bq * S, D), w_rg[l], b_rg[l], w_re[l], b_re[l],
                         w_gate[l], w_up[l], w_down[l]).reshape(Bq, S, D)
    return rms_norm(x, g_final)
```

```python
import functools

import jax
import jax.numpy as jnp
from jax import lax
from jax.experimental import pallas as pl
from jax.experimental.pallas import tpu as pltpu

F32 = jnp.float32
BF16 = jnp.bfloat16

RMS_EPS = 1e-6
LN_EPS = 1e-5
NEG_INIT = -1e30

N_HEADS = 4
HEAD_DIM = 128
CHUNK = 128
CONV_WIDTH = 5
HALO = 8
N_GROUPS = 4
N_XHEADS = 4
N_EXPERT_GROUPS = 4
EXPERTS_PER_GROUP = 8
N_EXPERTS = N_EXPERT_GROUPS * EXPERTS_PER_GROUP
TOP_K = 2
LANES = 128

PROJ_TILE = 512
MIX_TILE = 512
EXPERT_BLOCK = 256
COMBINE_TILE = 256
CONV_ROWS = 512
VMEM_LIMIT = 56 * 1024 * 1024


def _dot(a, b):
    return jnp.dot(a, b, preferred_element_type=F32)


def _dot_nt(a, b):
    return lax.dot_general(a, b, (((1,), (1,)), ((), ())), preferred_element_type=F32)


def _rms(x, g):
    return x * lax.rsqrt(jnp.mean(x * x, axis=-1, keepdims=True) + RMS_EPS) * g


def _sigmoid(x):
    return 1.0 / (1.0 + jnp.exp(-x))


def _silu(x):
    return x * _sigmoid(x)


def _gelu_tanh(x):
    c = 0.7978845608028654
    return 0.5 * x * (1.0 + jnp.tanh(c * (x + 0.044715 * (x * x * x))))


def _log_sigmoid(x):
    return jnp.minimum(x, 0.0) - jnp.log(1.0 + jnp.exp(-jnp.abs(x)))


def _proj_in_kernel(x_ref, g_ref, w_ref, wgt_ref, gb_ref,
                    qk_ref, v_ref, o_ref, gu_ref, gv_ref, gt_ref):
    xb = _rms(x_ref[...], g_ref[...]).astype(BF16)
    qk_ref[...] = _dot(xb, w_ref[:, 0:1024])
    v_ref[...] = _dot(xb, w_ref[:, 1024:1536])
    o_ref[...] = _dot(xb, w_ref[:, 1536:2048])
    gu_ref[...] = _dot(xb, w_ref[:, 2048:2560])
    gv_ref[...] = _dot(xb, w_ref[:, 2560:3072])
    gt = _dot_nt(wgt_ref[...], xb) + gb_ref[...]
    for h in range(N_HEADS):
        for j in range(gt_ref.shape[1]):
            gt_ref[h, j] = gt[8 * h:8 * h + 8, CHUNK * j:CHUNK * (j + 1)]


def _proj_in(x, g_mix, w_main, w_gt, gate_b):
    B, S, D = x.shape
    tm = PROJ_TILE
    nj = tm // CHUNK
    NC = S // CHUNK
    T = B * S
    grid = (B, S // tm)
    row = lambda b, i: (b * (S // tm) + i, 0)
    const = lambda b, i: (0, 0)
    outs = pl.pallas_call(
        _proj_in_kernel,
        grid=grid,
        in_specs=[
            pl.BlockSpec((None, tm, D), lambda b, i: (b, i, 0)),
            pl.BlockSpec((1, D), const),
            pl.BlockSpec(w_main.shape, const),
            pl.BlockSpec(w_gt.shape, const),
            pl.BlockSpec(gate_b.shape, const),
        ],
        out_specs=[
            pl.BlockSpec((tm, 1024), row),
            pl.BlockSpec((tm, 512), row),
            pl.BlockSpec((tm, 512), row),
            pl.BlockSpec((tm, 512), row),
            pl.BlockSpec((tm, 512), row),
            pl.BlockSpec((None, N_HEADS, nj, 8, CHUNK), lambda b, i: (b, 0, i, 0, 0)),
        ],
        out_shape=[
            jax.ShapeDtypeStruct((T, 1024), F32),
            jax.ShapeDtypeStruct((T, 512), F32),
            jax.ShapeDtypeStruct((T, 512), F32),
            jax.ShapeDtypeStruct((T, 512), F32),
            jax.ShapeDtypeStruct((T, 512), F32),
            jax.ShapeDtypeStruct((B, N_HEADS, NC, 8, CHUNK), F32),
        ],
        compiler_params=pltpu.CompilerParams(
            dimension_semantics=("parallel", "parallel"), vmem_limit_bytes=VMEM_LIMIT),
        name="proj_in",
    )(x, g_mix, w_main, w_gt, gate_b)
    return outs


def _conv_silu(src_ref, w, bias, blk, n_blk, rows_per_blk):
    cpb = rows_per_blk // CHUNK
    parts = []
    if blk == 0:
        parts.append(jnp.zeros((HALO, HEAD_DIM), F32))
    else:
        parts.append(src_ref[blk * cpb - 1, CHUNK - HALO:CHUNK, :])
    for j in range(cpb):
        parts.append(src_ref[blk * cpb + j])
    if blk == n_blk - 1:
        parts.append(jnp.zeros((HALO, HEAD_DIM), F32))
    else:
        parts.append(src_ref[(blk + 1) * cpb, 0:HALO, :])
    ext = jnp.concatenate(parts, axis=0)
    n = rows_per_blk + 2 * HALO
    acc = jnp.zeros((rows_per_blk, HEAD_DIM), F32)
    for j in range(CONV_WIDTH):
        d = j - CONV_WIDTH // 2
        sh = ext if d == 0 else pltpu.roll(ext, (-d) % n, axis=0)
        acc = acc + w[j:j + 1, :] * sh[HALO:HALO + rows_per_blk]
    return _silu(acc + bias)


def _mlstm_kernel(q_ref, k_ref, v_ref, o_ref, g_ref, cwq_ref, cwk_ref, cbq_ref, cbk_ref, gh_ref,
                  out_ref, qc_ref, kt_ref, vaug_ref, caug_ref, m_ref):
    NC = q_ref.shape[0]
    S = NC * CHUNK
    rows = min(CONV_ROWS, S)
    n_blk = S // rows
    cpb = rows // CHUNK
    k_scale = HEAD_DIM ** -0.5

    cwq, cwk = cwq_ref[...], cwk_ref[...]
    cbq, cbk = cbq_ref[...], cbk_ref[...]
    for blk in range(n_blk):
        qb = _conv_silu(q_ref, cwq, cbq, blk, n_blk, rows)
        kb = _conv_silu(k_ref, cwk, cbk, blk, n_blk, rows) * k_scale
        for j in range(cpb):
            c = blk * cpb + j
            qc_ref[c] = qb[j * CHUNK:(j + 1) * CHUNK].astype(BF16)
            kt_ref[c] = kb[j * CHUNK:(j + 1) * CHUNK].T
    ones = jnp.ones((CHUNK, HEAD_DIM), BF16)
    for c in range(NC):
        vaug_ref[c, :, 0:HEAD_DIM] = v_ref[c].astype(BF16)
        vaug_ref[c, :, HEAD_DIM:2 * HEAD_DIM] = ones

    t_idx = lax.broadcasted_iota(jnp.int32, (CHUNK, CHUNK), 0)
    s_idx = lax.broadcasted_iota(jnp.int32, (CHUNK, CHUNK), 1)
    gh = gh_ref[...]

    def chunk_step(c, direction):
        g = g_ref[c]
        ls = _log_sigmoid(g)
        hi = ls.astype(BF16).astype(F32)
        r1 = ls - hi
        mid = r1.astype(BF16).astype(F32)
        lo = (r1 - mid).astype(BF16).astype(F32)
        parts = jnp.concatenate([hi, mid, lo, jnp.zeros_like(hi)], axis=0).astype(BF16)
        if direction == 0:
            tri = (t_idx <= s_idx)
            causal = s_idx <= t_idx
        else:
            tri = (t_idx >= s_idx)
            causal = s_idx >= t_idx
        cs = _dot(parts, jnp.where(tri, 1.0, 0.0).astype(BF16))
        cum = cs[0:8] + cs[8:16] + cs[16:24]
        ig = g[2 * direction:2 * direction + 1]
        b_r = cum[2 * direction + 1:2 * direction + 2]
        f_tot = b_r[:, CHUNK - 1:CHUNK] if direction == 0 else b_r[:, 0:1]
        r_r = ig - b_r
        a_r = f_tot + r_r
        m_loc = jnp.max(a_r, axis=1, keepdims=True)
        wk_r = jnp.exp(a_r - m_loc)
        b_c = jnp.broadcast_to(b_r, (CHUNK, CHUNK)).T

        m_prev = m_ref[0:1, 0:1]
        caug = caug_ref[...]
        q = qc_ref[c]
        kt = kt_ref[c]
        vaug = vaug_ref[c]

        d_mat = b_c + r_r
        m_intra = jnp.max(jnp.where(causal, d_mat, -jnp.inf), axis=1, keepdims=True)
        inter = b_c[:, 0:1] + m_prev
        m_t = jnp.maximum(inter, m_intra)
        w_mat = jnp.where(causal, jnp.exp(d_mat - m_t), 0.0)
        s_inter = jnp.exp(inter - m_t)
        scores = _dot(q, kt.astype(BF16))
        wqk = (w_mat * scores).astype(BF16)
        tot = _dot(wqk, vaug) + s_inter * _dot(q, caug.astype(BF16))
        num = tot[:, 0:HEAD_DIM]
        den = tot[:, HEAD_DIM:2 * HEAD_DIM]
        h = num / jnp.maximum(jnp.abs(den), jnp.exp(-m_t))

        kw = (kt * wk_r).astype(BF16)
        c_loc = _dot(kw, vaug)
        m_new = jnp.maximum(f_tot + m_prev, m_loc)
        s_old = jnp.exp(f_tot + m_prev - m_new)
        s_new = jnp.exp(m_loc - m_new)
        caug_ref[...] = s_old * caug + s_new * c_loc
        m_ref[...] = jnp.broadcast_to(m_new, m_ref.shape)
        return h

    def reset_state():
        caug_ref[...] = jnp.zeros_like(caug_ref)
        m_ref[...] = jnp.full(m_ref.shape, NEG_INIT, F32)

    reset_state()

    def fwd_body(c, carry):
        out_ref[c] = chunk_step(c, 0)
        return carry

    lax.fori_loop(0, NC, fwd_body, 0)

    reset_state()

    def bwd_body(i, carry):
        c = NC - 1 - i
        hs = out_ref[c] + chunk_step(c, 1)
        hs = hs * lax.rsqrt(jnp.mean(hs * hs, axis=-1, keepdims=True) + RMS_EPS) * gh
        out_ref[c] = _sigmoid(o_ref[c]) * hs
        return carry

    lax.fori_loop(0, NC, bwd_body, 0)


def _mlstm(qk, v, o, gt, conv_w, conv_b, g_head):
    B, NC = qk.shape[0], qk.shape[1]
    blk = lambda off: pl.BlockSpec((None, NC, CHUNK, HEAD_DIM), lambda b, h: (b, 0, 0, h + off))
    par = lambda r, off: pl.BlockSpec((r, HEAD_DIM), lambda b, h: (0, h + off))
    return pl.pallas_call(
        _mlstm_kernel,
        grid=(B, N_HEADS),
        in_specs=[
            blk(0), blk(N_HEADS), blk(0), blk(0),
            pl.BlockSpec((None, None, NC, 8, CHUNK), lambda b, h: (b, h, 0, 0, 0)),
            par(CONV_WIDTH, 0), par(CONV_WIDTH, N_HEADS), par(1, 0), par(1, N_HEADS), par(1, 0),
        ],
        out_specs=blk(0),
        out_shape=jax.ShapeDtypeStruct(v.shape, F32),
        scratch_shapes=[
            pltpu.VMEM((NC, CHUNK, HEAD_DIM), BF16),
            pltpu.VMEM((NC, HEAD_DIM, CHUNK), F32),
            pltpu.VMEM((NC, CHUNK, 2 * HEAD_DIM), BF16),
            pltpu.VMEM((HEAD_DIM, 2 * HEAD_DIM), F32),
            pltpu.VMEM((8, LANES), F32),
        ],
        compiler_params=pltpu.CompilerParams(
            dimension_semantics=("parallel", "parallel"), vmem_limit_bytes=VMEM_LIMIT),
        name="mlstm",
    )(qk, qk, v, o, gt, conv_w, conv_w, conv_b, conv_b, g_head)


def _kv_kernel(mem_ref, g_ref, wkt_ref, wv_ref, kt_ref, v_ref):
    mn = _rms(mem_ref[...], g_ref[...]).astype(BF16)
    for h in range(N_XHEADS):
        kt_ref[h] = _dot_nt(wkt_ref[h], mn).astype(BF16)
        v_ref[h] = _dot(mn, wv_ref[h]).astype(BF16)


def _kv_proj(mem, g_mem, w_kt, w_v):
    B, M, D = mem.shape
    dh = D // N_XHEADS
    return pl.pallas_call(
        _kv_kernel,
        grid=(B,),
        in_specs=[
            pl.BlockSpec((None, M, D), lambda b: (b, 0, 0)),
            pl.BlockSpec((1, D), lambda b: (0, 0)),
            pl.BlockSpec(w_kt.shape, lambda b: (0, 0, 0)),
            pl.BlockSpec(w_v.shape, lambda b: (0, 0, 0)),
        ],
        out_specs=[
            pl.BlockSpec((None, N_XHEADS, dh, M), lambda b: (b, 0, 0, 0)),
            pl.BlockSpec((None, N_XHEADS, M, dh), lambda b: (b, 0, 0, 0)),
        ],
        out_shape=[
            jax.ShapeDtypeStruct((B, N_XHEADS, dh, M), BF16),
            jax.ShapeDtypeStruct((B, N_XHEADS, M, dh), BF16),
        ],
        compiler_params=pltpu.CompilerParams(
            dimension_semantics=("parallel",), vmem_limit_bytes=VMEM_LIMIT),
        name="kv_proj",
    )(mem, g_mem, w_kt, w_v)


def _mix_attn_kernel(x_ref, ym_ref, gu_ref, gv_ref, lng_ref, lnb_ref, ws_ref, bs_ref, wout_ref,
                     gx_ref, wq_ref, kt_ref, vm_ref, wo_ref, gmoe_ref, wr_ref, br_ref,
                     x2_ref, hn_ref, lg_ref):
    tm = x_ref.shape[0]
    gd = HEAD_DIM
    gu = _gelu_tanh(gu_ref[...])
    gv = _gelu_tanh(gv_ref[...])
    mu = jnp.mean(gv, axis=-1, keepdims=True)
    gc = gv - mu
    gvn = gc * lax.rsqrt(jnp.mean(gc * gc, axis=-1, keepdims=True) + LN_EPS) * lng_ref[...] + lnb_ref[...]
    gvb = gvn.astype(BF16)
    rows = []
    for j in range(tm // CHUNK):
        cols = []
        for g in range(N_GROUPS):
            sp = _dot(ws_ref[g], gvb[j * CHUNK:(j + 1) * CHUNK, g * gd:(g + 1) * gd]) + bs_ref[g]
            cols.append(sp)
        rows.append(jnp.concatenate(cols, axis=1))
    y_gmlp = gu * jnp.concatenate(rows, axis=0)
    n_ml = ym_ref.shape[1]
    mix = _dot(ym_ref[...].astype(BF16), wout_ref[0:n_ml, :]) + _dot(y_gmlp.astype(BF16), wout_ref[n_ml:, :])
    x1 = x_ref[...] + mix

    hq = _rms(x1, gx_ref[...]).astype(BF16)
    q = _dot(hq, wq_ref[...])
    dh = kt_ref.shape[1]
    scale = dh ** -0.5
    heads = []
    for h in range(N_XHEADS):
        s = _dot(q[:, h * dh:(h + 1) * dh].astype(BF16), kt_ref[h]) * scale
        e = jnp.exp(s - jnp.max(s, axis=-1, keepdims=True))
        p = e / jnp.sum(e, axis=-1, keepdims=True)
        heads.append(_dot(p.astype(BF16), vm_ref[h]))
    att = jnp.concatenate(heads, axis=1).astype(BF16)
    x2 = x1 + _dot(att, wo_ref[...])
    x2_ref[...] = x2

    hn = _rms(x2, gmoe_ref[...])
    hn_ref[...] = hn
    lg_ref[...] = _dot(hn.astype(BF16), wr_ref[...]) + br_ref[...]


def _mix_attn(x, ym, gu, gv, ln_g, ln_b, w_s, b_s, w_out, g_x, w_q, kt, vm, w_o, g_moe, w_r, b_r, S):
    T, D = x.shape
    tm = MIX_TILE
    per_b = S // tm
    row = lambda i: (i, 0)
    c2 = lambda i: (0, 0)
    c3 = lambda i: (0, 0, 0)
    full = lambda a: pl.BlockSpec(a.shape, c2 if a.ndim == 2 else c3)
    return pl.pallas_call(
        _mix_attn_kernel,
        grid=(T // tm,),
        in_specs=[
            pl.BlockSpec((tm, D), row), pl.BlockSpec((tm, ym.shape[1]), row),
            pl.BlockSpec((tm, gu.shape[1]), row), pl.BlockSpec((tm, gv.shape[1]), row),
            full(ln_g), full(ln_b), full(w_s), full(b_s), full(w_out),
            full(g_x), full(w_q),
            pl.BlockSpec((None,) + kt.shape[1:], lambda i: (i // per_b, 0, 0, 0)),
            pl.BlockSpec((None,) + vm.shape[1:], lambda i: (i // per_b, 0, 0, 0)),
            full(w_o), full(g_moe), full(w_r), full(b_r),
        ],
        out_specs=[pl.BlockSpec((tm, D), row), pl.BlockSpec((tm, D), row), pl.BlockSpec((tm, LANES), row)],
        out_shape=[
            jax.ShapeDtypeStruct((T, D), F32),
            jax.ShapeDtypeStruct((T, D), F32),
            jax.ShapeDtypeStruct((T, LANES), F32),
        ],
        compiler_params=pltpu.CompilerParams(
            dimension_semantics=("parallel",), vmem_limit_bytes=VMEM_LIMIT),
        name="mix_attn",
    )(x, ym, gu, gv, ln_g, ln_b, w_s, b_s, w_out, g_x, w_q, kt, vm, w_o, g_moe, w_r, b_r)


def _row_copy(src_hbm, idx, buf, slot, r, sem):
    return pltpu.make_async_copy(src_hbm.at[pl.ds(idx, 1), :], buf.at[slot, pl.ds(r, 1), :], sem.at[slot])


def _idx_copy(idx_hbm, block, idx_smem, slot, isem):
    return pltpu.make_async_copy(idx_hbm.at[block], idx_smem.at[slot], isem.at[slot])


def _gather_pipeline(i, nb, idx_hbm, idx_smem, isem, issue_rows, wait_rows):
    slot = lax.rem(i, 2)

    @pl.when(i == 0)
    def _():
        first = _idx_copy(idx_hbm, 0, idx_smem, 0, isem)
        first.start()
        first.wait()
        issue_rows(0)

        @pl.when(nb > 1)
        def _():
            _idx_copy(idx_hbm, 1, idx_smem, 1, isem).start()

    @pl.when(i + 1 < nb)
    def _():
        _idx_copy(idx_hbm, i + 1, idx_smem, 1 - slot, isem).wait()
        issue_rows(1 - slot)

    @pl.when(i + 2 < nb)
    def _():
        _idx_copy(idx_hbm, i + 2, idx_smem, slot, isem).start()

    wait_rows(slot)
    return slot


def _expert_kernel(blk_e_ref, tok_hbm, h_hbm, wgu_ref, wd_ref, y_ref, buf, idx, sem, isem):
    i = pl.program_id(0)
    nb = pl.num_programs(0)
    R = buf.shape[1]

    def issue_rows(s):
        def body(r, carry):
            _row_copy(h_hbm, idx[s, r], buf, s, r, sem).start()
            return carry
        lax.fori_loop(0, R, body, 0, unroll=8)

    def wait_rows(s):
        def body(r, carry):
            _row_copy(h_hbm, 0, buf, s, r, sem).wait()
            return carry
        lax.fori_loop(0, R, body, 0, unroll=8)

    slot = _gather_pipeline(i, nb, tok_hbm, idx, isem, issue_rows, wait_rows)

    xb = buf[slot].astype(BF16)
    gu = _dot(xb, wgu_ref[...])
    de = wd_ref.shape[0]
    act = (_silu(gu[:, 0:de]) * gu[:, de:2 * de]).astype(BF16)
    y_ref[...] = _dot(act, wd_ref[...])


def _experts(blk_e, tok_of_slot, hn, w_gu, w_d):
    T, D = hn.shape
    nb, R = tok_of_slot.shape
    de = w_d.shape[1]
    return pl.pallas_call(
        _expert_kernel,
        grid_spec=pltpu.PrefetchScalarGridSpec(
            num_scalar_prefetch=1,
            grid=(nb,),
            in_specs=[
                pl.BlockSpec(memory_space=pl.ANY),
                pl.BlockSpec(memory_space=pl.ANY),
                pl.BlockSpec((None, D, 2 * de), lambda i, be: (be[i], 0, 0)),
                pl.BlockSpec((None, de, D), lambda i, be: (be[i], 0, 0)),
            ],
            out_specs=pl.BlockSpec((R, D), lambda i, be: (i, 0)),
            scratch_shapes=[pltpu.VMEM((2, R, D), F32), pltpu.SMEM((2, R), jnp.int32),
                            pltpu.SemaphoreType.DMA((2,)), pltpu.SemaphoreType.DMA((2,))],
        ),
        out_shape=jax.ShapeDtypeStruct((nb * R, D), F32),
        compiler_params=pltpu.CompilerParams(
            dimension_semantics=("arbitrary",), vmem_limit_bytes=VMEM_LIMIT),
        name="experts",
    )(blk_e, tok_of_slot, hn, w_gu, w_d)


def _combine_kernel(dd_hbm, y_hbm, x2_ref, w0_ref, w1_ref, gf_ref, out_ref, buf, idx, sem, isem):
    i = pl.program_id(0)
    nb = pl.num_programs(0)
    R = x2_ref.shape[0]

    def issue_rows(s):
        def body(r, carry):
            _row_copy(y_hbm, idx[s, r], buf, s, r, sem).start()
            return carry
        lax.fori_loop(0, 2 * R, body, 0, unroll=8)

    def wait_rows(s):
        def body(r, carry):
            _row_copy(y_hbm, 0, buf, s, r, sem).wait()
            return carry
        lax.fori_loop(0, 2 * R, body, 0, unroll=8)

    slot = _gather_pipeline(i, nb, dd_hbm, idx, isem, issue_rows, wait_rows)

    moe = w0_ref[...] * buf[slot, 0:R, :] + w1_ref[...] * buf[slot, R:2 * R, :]
    out_ref[...] = _rms(x2_ref[...] + moe, gf_ref[...])


def _combine(dd, y_disp, x2, w0, w1, g_final):
    T, D = x2.shape
    R = COMBINE_TILE
    row = lambda i: (i, 0)
    return pl.pallas_call(
        _combine_kernel,
        grid=(T // R,),
        in_specs=[
            pl.BlockSpec(memory_space=pl.ANY),
            pl.BlockSpec(memory_space=pl.ANY),
            pl.BlockSpec((R, D), row),
            pl.BlockSpec((R, 1), row),
            pl.BlockSpec((R, 1), row),
            pl.BlockSpec((1, D), lambda i: (0, 0)),
        ],
        out_specs=pl.BlockSpec((R, D), row),
        scratch_shapes=[pltpu.VMEM((2, 2 * R, D), F32), pltpu.SMEM((2, 2 * R), jnp.int32),
                        pltpu.SemaphoreType.DMA((2,)), pltpu.SemaphoreType.DMA((2,))],
        out_shape=jax.ShapeDtypeStruct((T, D), F32),
        compiler_params=pltpu.CompilerParams(
            dimension_semantics=("arbitrary",), vmem_limit_bytes=VMEM_LIMIT),
        name="combine",
    )(dd, y_disp, x2, w0, w1, g_final)


def _route(logits, T):
    g_logits = logits[:, 0:N_EXPERT_GROUPS]
    p_group = jax.nn.softmax(g_logits, axis=-1)
    p_top, g_idx = lax.top_k(p_group, 1)
    e_logits = logits[:, N_EXPERT_GROUPS:N_EXPERT_GROUPS + N_EXPERTS].reshape(T, N_EXPERT_GROUPS, EXPERTS_PER_GROUP)
    e_logits = jnp.take_along_axis(e_logits, g_idx[:, :, None], axis=1)[:, 0]
    e_top, e_local = lax.top_k(e_logits, TOP_K)
    weights = p_top * jax.nn.softmax(e_top, axis=-1)
    expert_id = g_idx * EXPERTS_PER_GROUP + e_local

    A = T * TOP_K
    R = EXPERT_BLOCK
    e_flat = expert_id.reshape(A).astype(jnp.int32)
    t_flat = jnp.repeat(jnp.arange(T, dtype=jnp.int32), TOP_K)
    order = jnp.argsort(e_flat)
    e_sorted, t_sorted = e_flat[order], t_flat[order]
    counts = jnp.bincount(e_flat, length=N_EXPERTS).astype(jnp.int32)
    padded = ((counts + R - 1) // R) * R
    start = jnp.cumsum(counts) - counts
    pend = jnp.cumsum(padded)
    pstart = pend - padded
    dest = pstart[e_sorted] + (jnp.arange(A, dtype=jnp.int32) - start[e_sorted])
    nb = A // R + N_EXPERTS
    P = nb * R
    tok_of_slot = jnp.zeros((P,), jnp.int32).at[dest].set(t_sorted)
    blk_e = jnp.clip(jnp.searchsorted(pend, jnp.arange(nb, dtype=jnp.int32) * R, side='right'),
                     0, N_EXPERTS - 1).astype(jnp.int32)
    dest_of_a = jnp.zeros((A,), jnp.int32).at[order].set(dest)
    d = dest_of_a.reshape(T // COMBINE_TILE, COMBINE_TILE, TOP_K)
    dd = jnp.transpose(d, (0, 2, 1)).reshape(T // COMBINE_TILE, TOP_K * COMBINE_TILE)
    return blk_e, tok_of_slot.reshape(nb, R), dd, weights[:, 0:1], weights[:, 1:2]


def kernel(x, mem, g_mix, w_in, conv_w, conv_b, gate_b, g_head, ln_v_g, ln_v_b, w_s, b_s, w_out,
           g_xattn, g_mem, w_q_x, w_kv_x, w_o_x, g_moe, w_rg, b_rg, w_re, b_re,
           w_gate, w_up, w_down, g_final):
    B, S, D = x.shape
    T = B * S
    NC = S // CHUNK
    d_ml = N_HEADS * HEAD_DIM
    d_gm = N_GROUPS * HEAD_DIM
    n_gates = 4 * N_HEADS
    assert w_in.shape[0] == 1, "one layer"
    assert S % PROJ_TILE == 0 and S % MIX_TILE == 0 and T % COMBINE_TILE == 0
    assert (T * TOP_K) % EXPERT_BLOCK == 0

    w = w_in[0]
    c_g = 4 * d_ml
    w_main = jnp.concatenate([w[:, 0:c_g], w[:, c_g + n_gates:]], axis=1).astype(BF16)
    wg = w[:, c_g:c_g + n_gates].reshape(D, 4, N_HEADS)
    wg = jnp.pad(jnp.transpose(wg, (2, 1, 0)), ((0, 0), (0, 4), (0, 0)))
    w_gt = wg.reshape(N_HEADS * 8, D).astype(BF16)
    gb = jnp.pad(jnp.transpose(gate_b[0].reshape(4, N_HEADS), (1, 0)), ((0, 0), (0, 4)))
    gb = gb.reshape(N_HEADS * 8, 1).astype(F32)

    qk, v, o, gu, gv, gt = _proj_in(x, g_mix[0:1], w_main, w_gt, gb)

    y_ml = _mlstm(qk.reshape(B, NC, CHUNK, 2 * d_ml), v.reshape(B, NC, CHUNK, d_ml),
                  o.reshape(B, NC, CHUNK, d_ml), gt,
                  conv_w[0].reshape(CONV_WIDTH, 2 * d_ml), conv_b[0:1], g_head[0:1])
    y_ml = y_ml.reshape(T, d_ml)

    dh = D // N_XHEADS
    M = mem.shape[1]
    w_kv = w_kv_x[0]
    w_kt = jnp.transpose(w_kv[:, 0:D].reshape(D, N_XHEADS, dh), (1, 2, 0)).astype(BF16)
    w_v = jnp.transpose(w_kv[:, D:2 * D].reshape(D, N_XHEADS, dh), (1, 0, 2)).astype(BF16)
    kt, vm = _kv_proj(mem, g_mem[0:1], w_kt, w_v)

    w_r = jnp.concatenate([w_rg[0], w_re[0]], axis=1)
    w_r = jnp.pad(w_r, ((0, 0), (0, LANES - w_r.shape[1]))).astype(BF16)
    b_r = jnp.concatenate([b_rg[0], b_re[0]])
    b_r = jnp.pad(b_r, (0, LANES - b_r.shape[0])).reshape(1, LANES).astype(F32)
    bs_b = jnp.broadcast_to(b_s[0][:, :, None], (N_GROUPS, CHUNK, HEAD_DIM)).astype(F32)

    x2, hn, logits = _mix_attn(
        x.reshape(T, D), y_ml, gu, gv, ln_v_g[0:1], ln_v_b[0:1], w_s[0].astype(BF16), bs_b,
        w_out[0].astype(BF16), g_xattn[0:1], w_q_x[0].astype(BF16), kt, vm, w_o_x[0].astype(BF16),
        g_moe[0:1], w_r, b_r, S)

    blk_e, tok_of_slot, dd, w0, w1 = _route(logits, T)

    w_gu = jnp.concatenate([w_gate[0], w_up[0]], axis=2).astype(BF16)
    y_disp = _experts(blk_e, tok_of_slot, hn, w_gu, w_down[0].astype(BF16))

    out = _combine(dd, y_disp, x2, w0, w1, g_final.reshape(1, D))
    return out.reshape(B, S, D)
```

```python
import functools

import jax
import jax.numpy as jnp
from jax import lax
from jax.experimental import pallas as pl
from jax.experimental.pallas import tpu as pltpu

F32 = jnp.float32
BF16 = jnp.bfloat16

RMS_EPS = 1e-6
LN_EPS = 1e-5
NEG_INIT = -1e30

N_HEADS = 4
HEAD_DIM = 128
CHUNK = 128
CONV_WIDTH = 5
HALO = 8
N_GROUPS = 4
N_XHEADS = 4
N_EXPERT_GROUPS = 4
EXPERTS_PER_GROUP = 8
N_EXPERTS = N_EXPERT_GROUPS * EXPERTS_PER_GROUP
TOP_K = 2
LANES = 128

PROJ_TILE = 512
MIX_TILE = 512
EXPERT_BLOCK = 256
COMBINE_TILE = 256
CONV_ROWS = 512
VMEM_LIMIT = 56 * 1024 * 1024


def _dot(a, b):
    return jnp.dot(a, b, preferred_element_type=F32)


def _dot_nt(a, b):
    return lax.dot_general(a, b, (((1,), (1,)), ((), ())), preferred_element_type=F32)


def _rms(x, g):
    return x * lax.rsqrt(jnp.mean(x * x, axis=-1, keepdims=True) + RMS_EPS) * g


def _sigmoid(x):
    return 1.0 / (1.0 + jnp.exp(-x))


def _silu(x):
    return x * _sigmoid(x)


def _gelu_tanh(x):
    c = 0.7978845608028654
    return 0.5 * x * (1.0 + jnp.tanh(c * (x + 0.044715 * (x * x * x))))


def _log_sigmoid(x):
    return jnp.minimum(x, 0.0) - jnp.log(1.0 + jnp.exp(-jnp.abs(x)))


def _proj_in_kernel(x_ref, g_ref, w_ref, wgt_ref, gb_ref,
                    qk_ref, v_ref, o_ref, gu_ref, gv_ref, gt_ref):
    xb = _rms(x_ref[...], g_ref[...]).astype(BF16)
    qk_ref[...] = _dot(xb, w_ref[:, 0:1024])
    v_ref[...] = _dot(xb, w_ref[:, 1024:1536])
    o_ref[...] = _dot(xb, w_ref[:, 1536:2048])
    gu_ref[...] = _dot(xb, w_ref[:, 2048:2560])
    gv_ref[...] = _dot(xb, w_ref[:, 2560:3072])
    gt = _dot_nt(wgt_ref[...], xb) + gb_ref[...]
    for h in range(N_HEADS):
        for j in range(gt_ref.shape[1]):
            gt_ref[h, j] = gt[8 * h:8 * h + 8, CHUNK * j:CHUNK * (j + 1)]


def _proj_in(x, g_mix, w_main, w_gt, gate_b):
    B, S, D = x.shape
    tm = PROJ_TILE
    nj = tm // CHUNK
    NC = S // CHUNK
    T = B * S
    grid = (B, S // tm)
    row = lambda b, i: (b * (S // tm) + i, 0)
    const = lambda b, i: (0, 0)
    outs = pl.pallas_call(
        _proj_in_kernel,
        grid=grid,
        in_specs=[
            pl.BlockSpec((None, tm, D), lambda b, i: (b, i, 0)),
            pl.BlockSpec((1, D), const),
            pl.BlockSpec(w_main.shape, const),
            pl.BlockSpec(w_gt.shape, const),
            pl.BlockSpec(gate_b.shape, const),
        ],
        out_specs=[
            pl.BlockSpec((tm, 1024), row),
            pl.BlockSpec((tm, 512), row),
            pl.BlockSpec((tm, 512), row),
            pl.BlockSpec((tm, 512), row),
            pl.BlockSpec((tm, 512), row),
            pl.BlockSpec((None, N_HEADS, nj, 8, CHUNK), lambda b, i: (b, 0, i, 0, 0)),
        ],
        out_shape=[
            jax.ShapeDtypeStruct((T, 1024), F32),
            jax.ShapeDtypeStruct((T, 512), F32),
            jax.ShapeDtypeStruct((T, 512), F32),
            jax.ShapeDtypeStruct((T, 512), F32),
            jax.ShapeDtypeStruct((T, 512), F32),
            jax.ShapeDtypeStruct((B, N_HEADS, NC, 8, CHUNK), F32),
        ],
        compiler_params=pltpu.CompilerParams(
            dimension_semantics=("parallel", "parallel"), vmem_limit_bytes=VMEM_LIMIT),
        name="proj_in",
    )(x, g_mix, w_main, w_gt, gate_b)
    return outs


def _conv_silu(src_ref, w, bias, blk, n_blk, rows_per_blk):
    cpb = rows_per_blk // CHUNK
    parts = []
    if blk == 0:
        parts.append(jnp.zeros((HALO, HEAD_DIM), F32))
    else:
        parts.append(src_ref[blk * cpb - 1, CHUNK - HALO:CHUNK, :])
    for j in range(cpb):
        parts.append(src_ref[blk * cpb + j])
    if blk == n_blk - 1:
        parts.append(jnp.zeros((HALO, HEAD_DIM), F32))
    else:
        parts.append(src_ref[(blk + 1) * cpb, 0:HALO, :])
    ext = jnp.concatenate(parts, axis=0)
    n = rows_per_blk + 2 * HALO
    acc = jnp.zeros((rows_per_blk, HEAD_DIM), F32)
    for j in range(CONV_WIDTH):
        d = j - CONV_WIDTH // 2
        sh = ext if d == 0 else pltpu.roll(ext, (-d) % n, axis=0)
        acc = acc + w[j:j + 1, :] * sh[HALO:HALO + rows_per_blk]
    return _silu(acc + bias)


def _mlstm_kernel(q_ref, k_ref, v_ref, o_ref, g_ref, cwq_ref, cwk_ref, cbq_ref, cbk_ref, gh_ref,
                  out_ref, qc_ref, kt_ref, vaug_ref, caug_ref, m_ref):
    NC = q_ref.shape[0]
    S = NC * CHUNK
    rows = min(CONV_ROWS, S)
    n_blk = S // rows
    cpb = rows // CHUNK
    k_scale = HEAD_DIM ** -0.5

    cwq, cwk = cwq_ref[...], cwk_ref[...]
    cbq, cbk = cbq_ref[...], cbk_ref[...]
    for blk in range(n_blk):
        qb = _conv_silu(q_ref, cwq, cbq, blk, n_blk, rows)
        kb = _conv_silu(k_ref, cwk, cbk, blk, n_blk, rows) * k_scale
        for j in range(cpb):
            c = blk * cpb + j
            qc_ref[c] = qb[j * CHUNK:(j + 1) * CHUNK].astype(BF16)
            kt_ref[c] = kb[j * CHUNK:(j + 1) * CHUNK].T
    ones = jnp.ones((CHUNK, HEAD_DIM), BF16)
    for c in range(NC):
        vaug_ref[c, :, 0:HEAD_DIM] = v_ref[c].astype(BF16)
        vaug_ref[c, :, HEAD_DIM:2 * HEAD_DIM] = ones

    t_idx = lax.broadcasted_iota(jnp.int32, (CHUNK, CHUNK), 0)
    s_idx = lax.broadcasted_iota(jnp.int32, (CHUNK, CHUNK), 1)
    gh = gh_ref[...]

    def chunk_step(c, direction):
        g = g_ref[c]
        ls = _log_sigmoid(g)
        hi = ls.astype(BF16).astype(F32)
        r1 = ls - hi
        mid = r1.astype(BF16).astype(F32)
        lo = (r1 - mid).astype(BF16).astype(F32)
        parts = jnp.concatenate([hi, mid, lo, jnp.zeros_like(hi)], axis=0).astype(BF16)
        if direction == 0:
            tri = (t_idx <= s_idx)
            causal = s_idx <= t_idx
        else:
            tri = (t_idx >= s_idx)
            causal = s_idx >= t_idx
        cs = _dot(parts, jnp.where(tri, 1.0, 0.0).astype(BF16))
        cum = cs[0:8] + cs[8:16] + cs[16:24]
        ig = g[2 * direction:2 * direction + 1]
        b_r = cum[2 * direction + 1:2 * direction + 2]
        f_tot = b_r[:, CHUNK - 1:CHUNK] if direction == 0 else b_r[:, 0:1]
        r_r = ig - b_r
        a_r = f_tot + r_r
        m_loc = jnp.max(a_r, axis=1, keepdims=True)
        wk_r = jnp.exp(a_r - m_loc)
        b_c = jnp.broadcast_to(b_r, (CHUNK, CHUNK)).T

        m_prev = m_ref[0:1, 0:1]
        caug = caug_ref[...]
        q = qc_ref[c]
        kt = kt_ref[c]
        vaug = vaug_ref[c]

        d_mat = b_c + r_r
        m_intra = jnp.max(jnp.where(causal, d_mat, -jnp.inf), axis=1, keepdims=True)
        inter = b_c[:, 0:1] + m_prev
        m_t = jnp.maximum(inter, m_intra)
        w_mat = jnp.where(causal, jnp.exp(d_mat - m_t), 0.0)
        s_inter = jnp.exp(inter - m_t)
        scores = _dot(q, kt.astype(BF16))
        wqk = (w_mat * scores).astype(BF16)
        tot = _dot(wqk, vaug) + s_inter * _dot(q, caug.astype(BF16))
        num = tot[:, 0:HEAD_DIM]
        den = tot[:, HEAD_DIM:2 * HEAD_DIM]
        h = num / jnp.maximum(jnp.abs(den), jnp.exp(-m_t))

        kw = (kt * wk_r).astype(BF16)
        c_loc = _dot(kw, vaug)
        m_new = jnp.maximum(f_tot + m_prev, m_loc)
        s_old = jnp.exp(f_tot + m_prev - m_new)
        s_new = jnp.exp(m_loc - m_new)
        caug_ref[...] = s_old * caug + s_new * c_loc
        m_ref[...] = jnp.broadcast_to(m_new, m_ref.shape)
        return h

    def reset_state():
        caug_ref[...] = jnp.zeros_like(caug_ref)
        m_ref[...] = jnp.full(m_ref.shape, NEG_INIT, F32)

    reset_state()

    def fwd_body(c, carry):
        out_ref[c] = chunk_step(c, 0)
        return carry

    lax.fori_loop(0, NC, fwd_body, 0)

    reset_state()

    def bwd_body(i, carry):
        c = NC - 1 - i
        hs = out_ref[c] + chunk_step(c, 1)
        hs = hs * lax.rsqrt(jnp.mean(hs * hs, axis=-1, keepdims=True) + RMS_EPS) * gh
        out_ref[c] = _sigmoid(o_ref[c]) * hs
        return carry

    lax.fori_loop(0, NC, bwd_body, 0)


def _mlstm(qk, v, o, gt, conv_w, conv_b, g_head):
    B, NC = qk.shape[0], qk.shape[1]
    blk = lambda off: pl.BlockSpec((None, NC, CHUNK, HEAD_DIM), lambda b, h: (b, 0, 0, h + off))
    par = lambda r, off: pl.BlockSpec((r, HEAD_DIM), lambda b, h: (0, h + off))
    return pl.pallas_call(
        _mlstm_kernel,
        grid=(B, N_HEADS),
        in_specs=[
            blk(0), blk(N_HEADS), blk(0), blk(0),
            pl.BlockSpec((None, None, NC, 8, CHUNK), lambda b, h: (b, h, 0, 0, 0)),
            par(CONV_WIDTH, 0), par(CONV_WIDTH, N_HEADS), par(1, 0), par(1, N_HEADS), par(1, 0),
        ],
        out_specs=blk(0),
        out_shape=jax.ShapeDtypeStruct(v.shape, F32),
        scratch_shapes=[
            pltpu.VMEM((NC, CHUNK, HEAD_DIM), BF16),
            pltpu.VMEM((NC, HEAD_DIM, CHUNK), F32),
            pltpu.VMEM((NC, CHUNK, 2 * HEAD_DIM), BF16),
            pltpu.VMEM((HEAD_DIM, 2 * HEAD_DIM), F32),
            pltpu.VMEM((8, LANES), F32),
        ],
        compiler_params=pltpu.CompilerParams(
            dimension_semantics=("parallel", "parallel"), vmem_limit_bytes=VMEM_LIMIT),
        name="mlstm",
    )(qk, qk, v, o, gt, conv_w, conv_w, conv_b, conv_b, g_head)


def _kv_kernel(mem_ref, g_ref, wkt_ref, wv_ref, kt_ref, v_ref):
    mn = _rms(mem_ref[...], g_ref[...]).astype(BF16)
    for h in range(N_XHEADS):
        kt_ref[h] = _dot_nt(wkt_ref[h], mn).astype(BF16)
        v_ref[h] = _dot(mn, wv_ref[h]).astype(BF16)


def _kv_proj(mem, g_mem, w_kt, w_v):
    B, M, D = mem.shape
    dh = D // N_XHEADS
    return pl.pallas_call(
        _kv_kernel,
        grid=(B,),
        in_specs=[
            pl.BlockSpec((None, M, D), lambda b: (b, 0, 0)),
            pl.BlockSpec((1, D), lambda b: (0, 0)),
            pl.BlockSpec(w_kt.shape, lambda b: (0, 0, 0)),
            pl.BlockSpec(w_v.shape, lambda b: (0, 0, 0)),
        ],
        out_specs=[
            pl.BlockSpec((None, N_XHEADS, dh, M), lambda b: (b, 0, 0, 0)),
            pl.BlockSpec((None, N_XHEADS, M, dh), lambda b: (b, 0, 0, 0)),
        ],
        out_shape=[
            jax.ShapeDtypeStruct((B, N_XHEADS, dh, M), BF16),
            jax.ShapeDtypeStruct((B, N_XHEADS, M, dh), BF16),
        ],
        compiler_params=pltpu.CompilerParams(
            dimension_semantics=("parallel",), vmem_limit_bytes=VMEM_LIMIT),
        name="kv_proj",
    )(mem, g_mem, w_kt, w_v)


def _mix_attn_kernel(x_ref, ym_ref, gu_ref, gv_ref, lng_ref, lnb_ref, ws_ref, bs_ref, wout_ref,
                     gx_ref, wq_ref, kt_ref, vm_ref, wo_ref, gmoe_ref, wr_ref, br_ref, ltri_ref,
                     x2_ref, hn_ref, rt_ref, cnt_ref, run_ref):
    tm = x_ref.shape[0]
    gd = HEAD_DIM
    gu = _gelu_tanh(gu_ref[...])
    gv = _gelu_tanh(gv_ref[...])
    mu = jnp.mean(gv, axis=-1, keepdims=True)
    gc = gv - mu
    gvn = gc * lax.rsqrt(jnp.mean(gc * gc, axis=-1, keepdims=True) + LN_EPS) * lng_ref[...] + lnb_ref[...]
    gvb = gvn.astype(BF16)
    rows = []
    for j in range(tm // CHUNK):
        cols = []
        for g in range(N_GROUPS):
            sp = _dot(ws_ref[g], gvb[j * CHUNK:(j + 1) * CHUNK, g * gd:(g + 1) * gd]) + bs_ref[g]
            cols.append(sp)
        rows.append(jnp.concatenate(cols, axis=1))
    y_gmlp = gu * jnp.concatenate(rows, axis=0)
    n_ml = ym_ref.shape[1]
    mix = _dot(ym_ref[...].astype(BF16), wout_ref[0:n_ml, :]) + _dot(y_gmlp.astype(BF16), wout_ref[n_ml:, :])
    x1 = x_ref[...] + mix

    hq = _rms(x1, gx_ref[...]).astype(BF16)
    q = _dot(hq, wq_ref[...])
    dh = kt_ref.shape[1]
    scale = dh ** -0.5
    heads = []
    for h in range(N_XHEADS):
        s = _dot(q[:, h * dh:(h + 1) * dh].astype(BF16), kt_ref[h]) * scale
        e = jnp.exp(s - jnp.max(s, axis=-1, keepdims=True))
        p = e / jnp.sum(e, axis=-1, keepdims=True)
        heads.append(_dot(p.astype(BF16), vm_ref[h]))
    att = jnp.concatenate(heads, axis=1).astype(BF16)
    x2 = x1 + _dot(att, wo_ref[...])
    x2_ref[...] = x2

    hn = _rms(x2, gmoe_ref[...])
    hn_ref[...] = hn
    lg = _dot(hn.astype(BF16), wr_ref[...]) + br_ref[...]
    lane = lax.broadcasted_iota(jnp.int32, lg.shape, 1).astype(F32)
    n_lanes = float(LANES)
    gmask = lane < N_EXPERT_GROUPS
    gmax = jnp.max(jnp.where(gmask, lg, -jnp.inf), axis=1, keepdims=True)
    ge = jnp.where(gmask, jnp.exp(lg - gmax), 0.0)
    p = ge / jnp.sum(ge, axis=1, keepdims=True)
    p_top = jnp.max(p, axis=1, keepdims=True)
    g_idx = jnp.min(jnp.where(jnp.where(gmask, p, -1.0) == p_top, lane, n_lanes), axis=1, keepdims=True)
    lo = N_EXPERT_GROUPS + EXPERTS_PER_GROUP * g_idx
    el = jnp.where(lane >= lo, jnp.where(lane < lo + EXPERTS_PER_GROUP, lg, -jnp.inf), -jnp.inf)
    v1 = jnp.max(el, axis=1, keepdims=True)
    i1 = jnp.min(jnp.where(el == v1, lane, n_lanes), axis=1, keepdims=True)
    el2 = jnp.where(lane == i1, -jnp.inf, el)
    v2 = jnp.max(el2, axis=1, keepdims=True)
    i2 = jnp.min(jnp.where(el2 == v2, lane, n_lanes), axis=1, keepdims=True)
    ex2 = jnp.exp(v2 - v1)
    den = 1.0 + ex2
    w1 = p_top * (1.0 / den)
    w2 = p_top * (ex2 / den)
    e1 = i1 - N_EXPERT_GROUPS
    e2 = i2 - N_EXPERT_GROUPS

    @pl.when(pl.program_id(0) == 0)
    def _():
        run_ref[...] = jnp.zeros_like(run_ref)

    is1 = lane == e1
    is2 = lane == e2
    oh = jnp.where(is1, 1.0, jnp.where(is2, 1.0, 0.0))
    before = _dot(ltri_ref[...], oh.astype(BF16)) + run_ref[0:1, :]
    rank1 = jnp.sum(jnp.where(is1, before, 0.0), axis=1, keepdims=True)
    rank2 = jnp.sum(jnp.where(is2, before, 0.0), axis=1, keepdims=True)
    total = run_ref[0:1, :] + jnp.sum(oh, axis=0, keepdims=True)
    run_ref[...] = jnp.broadcast_to(total, run_ref.shape)
    cnt_ref[...] = jnp.broadcast_to(total, cnt_ref.shape)
    rt = jnp.where(lane == 0.0, w1, jnp.where(lane == 1.0, w2, jnp.where(lane == 2.0, e1, jnp.where(
        lane == 3.0, e2, jnp.where(lane == 4.0, rank1, jnp.where(lane == 5.0, rank2, 0.0))))))
    rt_ref[...] = rt


def _mix_attn(x, ym, gu, gv, ln_g, ln_b, w_s, b_s, w_out, g_x, w_q, kt, vm, w_o, g_moe, w_r, b_r, S):
    T, D = x.shape
    tm = MIX_TILE
    per_b = S // tm
    row = lambda i: (i, 0)
    c2 = lambda i: (0, 0)
    c3 = lambda i: (0, 0, 0)
    full = lambda a: pl.BlockSpec(a.shape, c2 if a.ndim == 2 else c3)
    ti = lax.broadcasted_iota(jnp.int32, (tm, tm), 0)
    si = lax.broadcasted_iota(jnp.int32, (tm, tm), 1)
    ltri = jnp.where(si < ti, 1.0, 0.0).astype(BF16)
    return pl.pallas_call(
        _mix_attn_kernel,
        grid=(T // tm,),
        in_specs=[
            pl.BlockSpec((tm, D), row), pl.BlockSpec((tm, ym.shape[1]), row),
            pl.BlockSpec((tm, gu.shape[1]), row), pl.BlockSpec((tm, gv.shape[1]), row),
            full(ln_g), full(ln_b), full(w_s), full(b_s), full(w_out),
            full(g_x), full(w_q),
            pl.BlockSpec((None,) + kt.shape[1:], lambda i: (i // per_b, 0, 0, 0)),
            pl.BlockSpec((None,) + vm.shape[1:], lambda i: (i // per_b, 0, 0, 0)),
            full(w_o), full(g_moe), full(w_r), full(b_r), full(ltri),
        ],
        out_specs=[pl.BlockSpec((tm, D), row), pl.BlockSpec((tm, D), row), pl.BlockSpec((tm, LANES), row),
                   pl.BlockSpec((8, LANES), c2)],
        out_shape=[
            jax.ShapeDtypeStruct((T, D), F32),
            jax.ShapeDtypeStruct((T, D), F32),
            jax.ShapeDtypeStruct((T, LANES), F32),
            jax.ShapeDtypeStruct((8, LANES), F32),
        ],
        scratch_shapes=[pltpu.VMEM((8, LANES), F32)],
        compiler_params=pltpu.CompilerParams(
            dimension_semantics=("arbitrary",), vmem_limit_bytes=VMEM_LIMIT),
        name="mix_attn",
    )(x, ym, gu, gv, ln_g, ln_b, w_s, b_s, w_out, g_x, w_q, kt, vm, w_o, g_moe, w_r, b_r, ltri)


def _row_copy(src_hbm, idx, buf, slot, r, sem):
    return pltpu.make_async_copy(src_hbm.at[pl.ds(idx, 1), :], buf.at[slot, pl.ds(r, 1), :], sem.at[slot])


def _idx_copy(idx_hbm, block, idx_smem, slot, isem):
    return pltpu.make_async_copy(idx_hbm.at[block], idx_smem.at[slot], isem.at[slot])


def _gather_pipeline(i, nb, idx_hbm, idx_smem, isem, issue_rows, wait_rows):
    slot = lax.rem(i, 2)

    @pl.when(i == 0)
    def _():
        first = _idx_copy(idx_hbm, 0, idx_smem, 0, isem)
        first.start()
        first.wait()
        issue_rows(0)

        @pl.when(nb > 1)
        def _():
            _idx_copy(idx_hbm, 1, idx_smem, 1, isem).start()

    @pl.when(i + 1 < nb)
    def _():
        _idx_copy(idx_hbm, i + 1, idx_smem, 1 - slot, isem).wait()
        issue_rows(1 - slot)

    @pl.when(i + 2 < nb)
    def _():
        _idx_copy(idx_hbm, i + 2, idx_smem, slot, isem).start()

    wait_rows(slot)
    return slot


def _dispatch_kernel(dd_hbm, h_hbm, xz_hbm, xd_hbm, idx, sem, isem):
    del xz_hbm
    i = pl.program_id(0)
    nb = pl.num_programs(0)
    R = idx.shape[1] // TOP_K
    slot = lax.rem(i, 2)

    def row_copy(tok, dst, s):
        return pltpu.make_async_copy(h_hbm.at[pl.ds(tok, 1), :], xd_hbm.at[pl.ds(dst, 1), :], sem.at[s])

    @pl.when(i == 0)
    def _():
        first = _idx_copy(dd_hbm, 0, idx, 0, isem)
        first.start()
        first.wait()

    @pl.when(i + 1 < nb)
    def _():
        _idx_copy(dd_hbm, i + 1, idx, 1 - slot, isem).start()

    def issue(r, carry):
        row_copy(i * R + r, idx[slot, r], slot).start()
        row_copy(i * R + r, idx[slot, R + r], slot).start()
        return carry
    lax.fori_loop(0, R, issue, 0, unroll=8)

    def drain(s):
        def body(r, carry):
            row_copy(0, 0, s).wait()
            return carry
        lax.fori_loop(0, TOP_K * R, body, 0, unroll=8)

    @pl.when(i > 0)
    def _():
        drain(1 - slot)

    @pl.when(i + 1 < nb)
    def _():
        _idx_copy(dd_hbm, i + 1, idx, 1 - slot, isem).wait()

    @pl.when(i == nb - 1)
    def _():
        drain(slot)


def _dispatch(dd, hn, n_slots):
    T, D = hn.shape
    nt, two_r = dd.shape
    any_spec = pl.BlockSpec(memory_space=pl.ANY)
    return pl.pallas_call(
        _dispatch_kernel,
        grid=(nt,),
        in_specs=[any_spec, any_spec, any_spec],
        out_specs=any_spec,
        scratch_shapes=[pltpu.SMEM((2, two_r), jnp.int32),
                        pltpu.SemaphoreType.DMA((2,)), pltpu.SemaphoreType.DMA((2,))],
        out_shape=jax.ShapeDtypeStruct((n_slots, D), hn.dtype),
        input_output_aliases={2: 0},
        compiler_params=pltpu.CompilerParams(dimension_semantics=("arbitrary",)),
        name="dispatch",
    )(dd, hn, jnp.zeros((n_slots, D), hn.dtype))


def _expert_kernel(blk_e_ref, n_used_ref, x_ref, wgu_ref, wd_ref, y_ref):
    used = pl.program_id(0) < n_used_ref[0]

    @pl.when(used)
    def _():
        xb = x_ref[...].astype(BF16)
        gu = _dot(xb, wgu_ref[...])
        de = wd_ref.shape[0]
        act = (_silu(gu[:, 0:de]) * gu[:, de:2 * de]).astype(BF16)
        y_ref[...] = _dot(act, wd_ref[...])

    @pl.when(jnp.logical_not(used))
    def _():
        y_ref[...] = jnp.zeros_like(y_ref)


def _experts(blk_e, n_used, x_disp, w_gu, w_d):
    P, D = x_disp.shape
    R = EXPERT_BLOCK
    nb = P // R
    de = w_d.shape[1]
    last = lambda i, nu: jnp.minimum(i, nu[0] - 1)
    return pl.pallas_call(
        _expert_kernel,
        grid_spec=pltpu.PrefetchScalarGridSpec(
            num_scalar_prefetch=2,
            grid=(nb,),
            in_specs=[
                pl.BlockSpec((R, D), lambda i, be, nu: (last(i, nu), 0)),
                pl.BlockSpec((None, D, 2 * de), lambda i, be, nu: (be[last(i, nu)], 0, 0)),
                pl.BlockSpec((None, de, D), lambda i, be, nu: (be[last(i, nu)], 0, 0)),
            ],
            out_specs=pl.BlockSpec((R, D), lambda i, be, nu: (i, 0)),
        ),
        out_shape=jax.ShapeDtypeStruct((P, D), F32),
        compiler_params=pltpu.CompilerParams(
            dimension_semantics=("arbitrary",), vmem_limit_bytes=VMEM_LIMIT),
        name="experts",
    )(blk_e, n_used, x_disp, w_gu, w_d)


def _combine_kernel(dd_hbm, y_hbm, x2_ref, w0_ref, w1_ref, gf_ref, out_ref, buf, idx, sem, isem):
    i = pl.program_id(0)
    nb = pl.num_programs(0)
    R = x2_ref.shape[0]

    def issue_rows(s):
        def body(r, carry):
            _row_copy(y_hbm, idx[s, r], buf, s, r, sem).start()
            return carry
        lax.fori_loop(0, 2 * R, body, 0, unroll=8)

    def wait_rows(s):
        def body(r, carry):
            _row_copy(y_hbm, 0, buf, s, r, sem).wait()
            return carry
        lax.fori_loop(0, 2 * R, body, 0, unroll=8)

    slot = _gather_pipeline(i, nb, dd_hbm, idx, isem, issue_rows, wait_rows)

    moe = w0_ref[...] * buf[slot, 0:R, :] + w1_ref[...] * buf[slot, R:2 * R, :]
    out_ref[...] = _rms(x2_ref[...] + moe, gf_ref[...])


def _combine(dd, y_disp, x2, w0, w1, g_final):
    T, D = x2.shape
    R = COMBINE_TILE
    row = lambda i: (i, 0)
    return pl.pallas_call(
        _combine_kernel,
        grid=(T // R,),
        in_specs=[
            pl.BlockSpec(memory_space=pl.ANY),
            pl.BlockSpec(memory_space=pl.ANY),
            pl.BlockSpec((R, D), row),
            pl.BlockSpec((R, 1), row),
            pl.BlockSpec((R, 1), row),
            pl.BlockSpec((1, D), lambda i: (0, 0)),
        ],
        out_specs=pl.BlockSpec((R, D), row),
        scratch_shapes=[pltpu.VMEM((2, 2 * R, D), F32), pltpu.SMEM((2, 2 * R), jnp.int32),
                        pltpu.SemaphoreType.DMA((2,)), pltpu.SemaphoreType.DMA((2,))],
        out_shape=jax.ShapeDtypeStruct((T, D), F32),
        compiler_params=pltpu.CompilerParams(
            dimension_semantics=("arbitrary",), vmem_limit_bytes=VMEM_LIMIT),
        name="combine",
    )(dd, y_disp, x2, w0, w1, g_final)


def _route_tables(route, counts_row, T):
    R = EXPERT_BLOCK
    counts = counts_row[0, 0:N_EXPERTS].astype(jnp.int32)
    padded = ((counts + R - 1) // R) * R
    pend = jnp.cumsum(padded)
    pstart = pend - padded
    nb = (T * TOP_K) // R + N_EXPERTS
    first_slot = jnp.arange(nb, dtype=jnp.int32) * R
    blk_e = jnp.sum((pend[None, :] <= first_slot[:, None]).astype(jnp.int32), axis=1)
    blk_e = jnp.minimum(blk_e, N_EXPERTS - 1)
    n_used = (pend[N_EXPERTS - 1:N_EXPERTS] // R).astype(jnp.int32)
    e = route[:, 2:4].astype(jnp.int32)
    rank = route[:, 4:6].astype(jnp.int32)
    sel = e[:, :, None] == jnp.arange(N_EXPERTS, dtype=jnp.int32)[None, None, :]
    dest = jnp.sum(jnp.where(sel, pstart[None, None, :], 0), axis=2) + rank
    d = dest.reshape(T // COMBINE_TILE, COMBINE_TILE, TOP_K)
    dd = jnp.transpose(d, (0, 2, 1)).reshape(T // COMBINE_TILE, TOP_K * COMBINE_TILE)
    return blk_e, n_used, dd, nb * R


def kernel(x, mem, g_mix, w_in, conv_w, conv_b, gate_b, g_head, ln_v_g, ln_v_b, w_s, b_s, w_out,
           g_xattn, g_mem, w_q_x, w_kv_x, w_o_x, g_moe, w_rg, b_rg, w_re, b_re,
           w_gate, w_up, w_down, g_final):
    B, S, D = x.shape
    T = B * S
    NC = S // CHUNK
    d_ml = N_HEADS * HEAD_DIM
    d_gm = N_GROUPS * HEAD_DIM
    n_gates = 4 * N_HEADS
    assert w_in.shape[0] == 1, "one layer"
    assert S % PROJ_TILE == 0 and S % MIX_TILE == 0 and T % COMBINE_TILE == 0
    assert (T * TOP_K) % EXPERT_BLOCK == 0

    w = w_in[0]
    c_g = 4 * d_ml
    w_main = jnp.concatenate([w[:, 0:c_g], w[:, c_g + n_gates:]], axis=1).astype(BF16)
    wg = w[:, c_g:c_g + n_gates].reshape(D, 4, N_HEADS)
    wg = jnp.pad(jnp.transpose(wg, (2, 1, 0)), ((0, 0), (0, 4), (0, 0)))
    w_gt = wg.reshape(N_HEADS * 8, D).astype(BF16)
    gb = jnp.pad(jnp.transpose(gate_b[0].reshape(4, N_HEADS), (1, 0)), ((0, 0), (0, 4)))
    gb = gb.reshape(N_HEADS * 8, 1).astype(F32)

    qk, v, o, gu, gv, gt = _proj_in(x, g_mix[0:1], w_main, w_gt, gb)

    y_ml = _mlstm(qk.reshape(B, NC, CHUNK, 2 * d_ml), v.reshape(B, NC, CHUNK, d_ml),
                  o.reshape(B, NC, CHUNK, d_ml), gt,
                  conv_w[0].reshape(CONV_WIDTH, 2 * d_ml), conv_b[0:1], g_head[0:1])
    y_ml = y_ml.reshape(T, d_ml)

    dh = D // N_XHEADS
    M = mem.shape[1]
    w_kv = w_kv_x[0]
    w_kt = jnp.transpose(w_kv[:, 0:D].reshape(D, N_XHEADS, dh), (1, 2, 0)).astype(BF16)
    w_v = jnp.transpose(w_kv[:, D:2 * D].reshape(D, N_XHEADS, dh), (1, 0, 2)).astype(BF16)
    kt, vm = _kv_proj(mem, g_mem[0:1], w_kt, w_v)

    w_r = jnp.concatenate([w_rg[0], w_re[0]], axis=1)
    w_r = jnp.pad(w_r, ((0, 0), (0, LANES - w_r.shape[1]))).astype(BF16)
    b_r = jnp.concatenate([b_rg[0], b_re[0]])
    b_r = jnp.pad(b_r, (0, LANES - b_r.shape[0])).reshape(1, LANES).astype(F32)
    bs_b = jnp.broadcast_to(b_s[0][:, :, None], (N_GROUPS, CHUNK, HEAD_DIM)).astype(F32)

    x2, hn, route, counts = _mix_attn(
        x.reshape(T, D), y_ml, gu, gv, ln_v_g[0:1], ln_v_b[0:1], w_s[0].astype(BF16), bs_b,
        w_out[0].astype(BF16), g_xattn[0:1], w_q_x[0].astype(BF16), kt, vm, w_o_x[0].astype(BF16),
        g_moe[0:1], w_r, b_r, S)

    blk_e, n_used, dd, n_slots = _route_tables(route, counts, T)
    x_disp = _dispatch(dd, hn, n_slots)

    w_gu = jnp.concatenate([w_gate[0], w_up[0]], axis=2).astype(BF16)
    y_disp = _experts(blk_e, n_used, x_disp, w_gu, w_down[0].astype(BF16))

    out = _combine(dd, y_disp, x2, route[:, 0:1], route[:, 1:2], g_final.reshape(1, D))
    return out.reshape(B, S, D)
```

```python
import jax
import jax.numpy as jnp
from jax import lax
from jax.experimental import pallas as pl
from jax.experimental.pallas import tpu as pltpu

F32 = jnp.float32
BF16 = jnp.bfloat16

RMS_EPS = 1e-6
LN_EPS = 1e-5
NEG_INIT = -1e30

N_HEADS = 4
HEAD_DIM = 128
CHUNK = 128
CONV_WIDTH = 5
HALO = 8
N_GROUPS = 4
N_XHEADS = 4
N_EXPERT_GROUPS = 4
EXPERTS_PER_GROUP = 8
N_EXPERTS = N_EXPERT_GROUPS * EXPERTS_PER_GROUP
TOP_K = 2
LANES = 128

PROJ_TILE = 512
MIX_TILE = 512
EXPERT_BLOCK = 256
COMBINE_TILE = 256
CONV_ROWS = 64
HEADS_PER_STEP = 2
MLSTM_UNROLL = 2
VMEM_LIMIT = 56 * 1024 * 1024


def _dot(a, b):
    return jnp.dot(a, b, preferred_element_type=F32)


def _dot_nt(a, b):
    return lax.dot_general(a, b, (((1,), (1,)), ((), ())), preferred_element_type=F32)


def _rms(x, g):
    return x * lax.rsqrt(jnp.mean(x * x, axis=-1, keepdims=True) + RMS_EPS) * g


def _sigmoid(x):
    return 1.0 / (1.0 + jnp.exp(-x))


def _silu(x):
    return x * _sigmoid(x)


def _gelu_tanh(x):
    c = 0.7978845608028654
    return 0.5 * x * (1.0 + jnp.tanh(c * (x + 0.044715 * (x * x * x))))


def _log_sigmoid(x):
    return jnp.minimum(x, 0.0) - jnp.log(1.0 + jnp.exp(-jnp.abs(x)))


def _proj_in_kernel(x_ref, g_ref, w_ref, wgt_ref, gb_ref,
                    qk_ref, v_ref, o_ref, gu_ref, gv_ref, gt_ref):
    xb = _rms(x_ref[...], g_ref[...]).astype(BF16)
    qk_ref[...] = _dot(xb, w_ref[:, 0:1024])
    v_ref[...] = _dot(xb, w_ref[:, 1024:1536]).astype(BF16)
    o_ref[...] = _dot(xb, w_ref[:, 1536:2048])
    gu_ref[...] = _dot(xb, w_ref[:, 2048:2560])
    gv_ref[...] = _dot(xb, w_ref[:, 2560:3072])
    gt = _dot_nt(wgt_ref[...], xb) + gb_ref[...]
    for h in range(N_HEADS):
        for j in range(gt_ref.shape[1]):
            gt_ref[h, j] = gt[8 * h:8 * h + 8, CHUNK * j:CHUNK * (j + 1)]


def _proj_in(x, g_mix, w_main, w_gt, gate_b):
    B, S, D = x.shape
    tm = PROJ_TILE
    nj = tm // CHUNK
    NC = S // CHUNK
    T = B * S
    grid = (B, S // tm)
    row = lambda b, i: (b * (S // tm) + i, 0)
    const = lambda b, i: (0, 0)
    outs = pl.pallas_call(
        _proj_in_kernel,
        grid=grid,
        in_specs=[
            pl.BlockSpec((None, tm, D), lambda b, i: (b, i, 0)),
            pl.BlockSpec((1, D), const),
            pl.BlockSpec(w_main.shape, const),
            pl.BlockSpec(w_gt.shape, const),
            pl.BlockSpec(gate_b.shape, const),
        ],
        out_specs=[
            pl.BlockSpec((tm, 1024), row),
            pl.BlockSpec((tm, 512), row),
            pl.BlockSpec((tm, 512), row),
            pl.BlockSpec((tm, 512), row),
            pl.BlockSpec((tm, 512), row),
            pl.BlockSpec((None, N_HEADS, nj, 8, CHUNK), lambda b, i: (b, 0, i, 0, 0)),
        ],
        out_shape=[
            jax.ShapeDtypeStruct((T, 1024), F32),
            jax.ShapeDtypeStruct((T, 512), BF16),
            jax.ShapeDtypeStruct((T, 512), F32),
            jax.ShapeDtypeStruct((T, 512), F32),
            jax.ShapeDtypeStruct((T, 512), F32),
            jax.ShapeDtypeStruct((B, N_HEADS, NC, 8, CHUNK), F32),
        ],
        compiler_params=pltpu.CompilerParams(
            dimension_semantics=("parallel", "parallel"), vmem_limit_bytes=VMEM_LIMIT),
        name="proj_in",
    )(x, g_mix, w_main, w_gt, gate_b)
    return outs


def _conv_silu(src_ref, w, bias, r0, n_rows):
    S = src_ref.shape[0]
    acc = None
    for j in range(CONV_WIDTH):
        d = j - CONV_WIDTH // 2
        if r0 + d < 0 or r0 + d + n_rows > S:
            blk = src_ref[r0:r0 + n_rows, :]
            ridx = lax.broadcasted_iota(jnp.int32, blk.shape, 0)
            sh = pltpu.roll(blk, (-d) % n_rows, axis=0)
            tap = jnp.where((ridx + d >= 0) & (ridx + d < n_rows), sh, 0.0)
        else:
            tap = src_ref[r0 + d:r0 + d + n_rows, :]
        term = w[j:j + 1, :] * tap
        acc = term if acc is None else acc + term
    return _silu(acc + bias)


G_B, G_R, G_CM, G_WK, G_F, G_ML = range(6)


def _gate_rows(g, t_idx, s_idx):
    n = g.shape[0]
    ls = _log_sigmoid(g)
    hi = ls.astype(BF16).astype(F32)
    r1 = ls - hi
    mid = r1.astype(BF16).astype(F32)
    lo = (r1 - mid).astype(BF16).astype(F32)
    parts = jnp.concatenate([hi, mid, lo], axis=0).astype(BF16)
    one = lambda m: jnp.where(m, 1.0, 0.0).astype(BF16)
    rhs = jnp.concatenate([one(t_idx <= s_idx), one(t_idx >= s_idx), jnp.ones((CHUNK, CHUNK), BF16)], axis=1)
    cs = _dot(parts, rhs)
    cum = cs[0:n] + cs[n:2 * n] + cs[2 * n:3 * n]
    up1 = lambda a: pltpu.roll(a, n - 1, axis=0)
    fwd_row = lax.rem(lax.broadcasted_iota(jnp.int32, (n, CHUNK), 0), 8) == 0
    lane = lax.broadcasted_iota(jnp.int32, (n, CHUNK), 1)
    b = jnp.where(fwd_row, up1(cum[:, 0:CHUNK]), up1(cum[:, CHUNK:2 * CHUNK]))
    r = g - b
    f = up1(cum[:, 2 * CHUNK:3 * CHUNK])
    a = f + r
    ml = jnp.broadcast_to(jnp.max(a, axis=1, keepdims=True), a.shape)
    wk = jnp.exp(a - ml)
    pm = r
    sm = r
    s = 1
    while s < CHUNK:
        pm = jnp.maximum(pm, jnp.where(lane >= s, pltpu.roll(pm, s, axis=1), -jnp.inf))
        sm = jnp.maximum(sm, jnp.where(lane < CHUNK - s, pltpu.roll(sm, CHUNK - s, axis=1), -jnp.inf))
        s *= 2
    cm = jnp.where(fwd_row, pm, sm)
    return b, r, cm, wk, f, ml


def _mlstm_kernel(q_ref, k_ref, v_ref, g_ref, cwq_ref, cwk_ref, cbq_ref, cbk_ref, gh_ref,
                  out_ref, qc_ref, kt_ref, gate_ref, caug_ref, m_ref):
    NC = v_ref.shape[0]
    hps = HEADS_PER_STEP
    k_scale = HEAD_DIM ** -0.5

    cwq, cwk = cwq_ref[...], cwk_ref[...]
    cbq, cbk = cbq_ref[...], cbk_ref[...]
    n_sub = CHUNK // CONV_ROWS
    for c in range(NC):
        qb = jnp.concatenate([_conv_silu(q_ref, cwq, cbq, c * CHUNK + i * CONV_ROWS, CONV_ROWS)
                              for i in range(n_sub)], axis=0)
        kb = jnp.concatenate([_conv_silu(k_ref, cwk, cbk, c * CHUNK + i * CONV_ROWS, CONV_ROWS)
                              for i in range(n_sub)], axis=0) * k_scale
        for hd in range(hps):
            cols = slice(hd * HEAD_DIM, (hd + 1) * HEAD_DIM)
            qc_ref[hd, c] = qb[:, cols].astype(BF16)
            kt_ref[hd, c] = kb[:, cols].T

    t_idx = lax.broadcasted_iota(jnp.int32, (CHUNK, CHUNK), 0)
    s_idx = lax.broadcasted_iota(jnp.int32, (CHUNK, CHUNK), 1)
    ones = jnp.ones((CHUNK, HEAD_DIM), BF16)
    gh = gh_ref[...]

    for hd in range(hps):
        rows = _gate_rows(g_ref[hd].reshape(NC * 8, CHUNK), t_idx, s_idx)
        for kind, val in enumerate(rows):
            gate_ref[kind, hd] = val.reshape(NC, 8, CHUNK)

    def col_form(row):
        return jnp.broadcast_to(row, (CHUNK, CHUNK)).T

    def chunk_step(hd, c, direction):
        st = 2 * hd + direction
        row = slice(2 * direction, 2 * direction + 1)
        b_r = gate_ref[G_B, hd, c][row]
        r_r = gate_ref[G_R, hd, c][row]
        cm_r = gate_ref[G_CM, hd, c][row]
        wk_r = gate_ref[G_WK, hd, c][row]
        f_tot = gate_ref[G_F, hd, c][row]
        m_loc = gate_ref[G_ML, hd, c][row]
        causal = (s_idx <= t_idx) if direction == 0 else (s_idx >= t_idx)

        m_prev = m_ref[st][0:1]
        caug = caug_ref[st]
        q = qc_ref[hd, c]
        kt = kt_ref[hd, c]
        vaug = jnp.concatenate([v_ref[c, :, hd * HEAD_DIM:(hd + 1) * HEAD_DIM], ones], axis=1)

        b_c = col_form(b_r)
        mx_c = jnp.maximum(col_form(cm_r), m_prev)
        w_mat = jnp.where(causal, jnp.exp(r_r - mx_c), 0.0)
        s_inter = jnp.exp(m_prev - mx_c)
        scores = _dot(q, kt.astype(BF16))
        wqk = (w_mat * scores).astype(BF16)
        intra = _dot(wqk, vaug)
        carried = _dot(q, caug.astype(BF16))
        num = intra[:, 0:HEAD_DIM] + s_inter * carried[:, 0:HEAD_DIM]
        den = intra[:, HEAD_DIM:2 * HEAD_DIM] + s_inter * carried[:, HEAD_DIM:2 * HEAD_DIM]
        h = num / jnp.maximum(jnp.abs(den), jnp.exp(-(b_c + mx_c)))

        kw = (kt * wk_r).astype(BF16)
        c_loc = _dot(kw, vaug)
        m_new = jnp.maximum(f_tot + m_prev, m_loc)
        s_old = jnp.exp(f_tot + m_prev - m_new)
        s_new = jnp.exp(m_loc - m_new)
        wide = lambda a: jnp.concatenate([a, a], axis=1)
        caug_ref[st] = wide(s_old) * caug + wide(s_new) * c_loc
        m_ref[st] = jnp.broadcast_to(m_new, m_ref.shape[1:])
        return h

    caug_ref[...] = jnp.zeros_like(caug_ref)
    m_ref[...] = jnp.full(m_ref.shape, NEG_INIT, F32)

    def finish(hd, hs):
        cols = slice(hd * HEAD_DIM, (hd + 1) * HEAD_DIM)
        return hs * lax.rsqrt(jnp.mean(hs * hs, axis=-1, keepdims=True) + RMS_EPS) * gh[:, cols]

    def make_body(final):
        def body(i, carry):
            cf = i
            cb = NC - 1 - i
            for hd in range(hps):
                cols = slice(hd * HEAD_DIM, (hd + 1) * HEAD_DIM)
                h_f = chunk_step(hd, cf, 0)
                h_b = chunk_step(hd, cb, 1)
                if final:
                    out_ref[cf, :, cols] = finish(hd, out_ref[cf, :, cols] + h_f)
                    out_ref[cb, :, cols] = finish(hd, out_ref[cb, :, cols] + h_b)
                else:
                    out_ref[cf, :, cols] = h_f
                    out_ref[cb, :, cols] = h_b
            return carry
        return body

    lax.fori_loop(0, NC // 2, make_body(False), 0, unroll=MLSTM_UNROLL)
    lax.fori_loop(NC // 2, NC, make_body(True), 0, unroll=MLSTM_UNROLL)


def _mlstm(qk, v, gt, conv_w, conv_b, g_head):
    B, S = qk.shape[0], qk.shape[1]
    NC = S // CHUNK
    assert NC % 2 == 0
    hps = HEADS_PER_STEP
    w = hps * HEAD_DIM
    k_off = N_HEADS // hps
    seq = lambda off: pl.BlockSpec((None, S, w), lambda b, p: (b, 0, p + off))
    blk = lambda off: pl.BlockSpec((None, NC, CHUNK, w), lambda b, p: (b, 0, 0, p + off))
    par = lambda r, off: pl.BlockSpec((r, w), lambda b, p: (0, p + off))
    return pl.pallas_call(
        _mlstm_kernel,
        grid=(B, N_HEADS // hps),
        in_specs=[
            seq(0), seq(k_off), blk(0),
            pl.BlockSpec((None, hps, NC, 8, CHUNK), lambda b, p: (b, p, 0, 0, 0)),
            par(CONV_WIDTH, 0), par(CONV_WIDTH, k_off), par(1, 0), par(1, k_off), par(1, 0),
        ],
        out_specs=blk(0),
        out_shape=jax.ShapeDtypeStruct(v.shape, F32),
        scratch_shapes=[
            pltpu.VMEM((hps, NC, CHUNK, HEAD_DIM), BF16),
            pltpu.VMEM((hps, NC, HEAD_DIM, CHUNK), F32),
            pltpu.VMEM((6, hps, NC, 8, CHUNK), F32),
            pltpu.VMEM((2 * hps, HEAD_DIM, 2 * HEAD_DIM), F32),
            pltpu.VMEM((2 * hps, 8, LANES), F32),
        ],
        compiler_params=pltpu.CompilerParams(
            dimension_semantics=("parallel", "parallel"), vmem_limit_bytes=VMEM_LIMIT),
        name="mlstm",
    )(qk, qk, v, gt, conv_w, conv_w, conv_b, conv_b, g_head)


def _kv_kernel(mem_ref, g_ref, wkt_ref, wv_ref, kt_ref, v_ref):
    mn = _rms(mem_ref[...], g_ref[...]).astype(BF16)
    for h in range(N_XHEADS):
        kt_ref[h] = _dot_nt(wkt_ref[h], mn).astype(BF16)
        v_ref[h] = _dot(mn, wv_ref[h]).astype(BF16)


def _kv_proj(mem, g_mem, w_kt, w_v):
    B, M, D = mem.shape
    dh = D // N_XHEADS
    return pl.pallas_call(
        _kv_kernel,
        grid=(B,),
        in_specs=[
            pl.BlockSpec((None, M, D), lambda b: (b, 0, 0)),
            pl.BlockSpec((1, D), lambda b: (0, 0)),
            pl.BlockSpec(w_kt.shape, lambda b: (0, 0, 0)),
            pl.BlockSpec(w_v.shape, lambda b: (0, 0, 0)),
        ],
        out_specs=[
            pl.BlockSpec((None, N_XHEADS, dh, M), lambda b: (b, 0, 0, 0)),
            pl.BlockSpec((None, N_XHEADS, M, dh), lambda b: (b, 0, 0, 0)),
        ],
        out_shape=[
            jax.ShapeDtypeStruct((B, N_XHEADS, dh, M), BF16),
            jax.ShapeDtypeStruct((B, N_XHEADS, M, dh), BF16),
        ],
        compiler_params=pltpu.CompilerParams(
            dimension_semantics=("parallel",), vmem_limit_bytes=VMEM_LIMIT),
        name="kv_proj",
    )(mem, g_mem, w_kt, w_v)


def _mix_attn_kernel(x_ref, hm_ref, o_ref, gu_ref, gv_ref, lng_ref, lnb_ref, ws_ref, bs_ref, wout_ref,
                     gx_ref, wq_ref, kt_ref, vm_ref, wo_ref, gmoe_ref, wr_ref, br_ref, ltri_ref,
                     x2_ref, hn_ref, rt_ref, cnt_ref, run_ref):
    tm = x_ref.shape[0]
    gd = HEAD_DIM
    y_mlstm = _sigmoid(o_ref[...]) * hm_ref[...]
    gu = _gelu_tanh(gu_ref[...])
    gv = _gelu_tanh(gv_ref[...])
    mu = jnp.mean(gv, axis=-1, keepdims=True)
    gc = gv - mu
    gvn = gc * lax.rsqrt(jnp.mean(gc * gc, axis=-1, keepdims=True) + LN_EPS) * lng_ref[...] + lnb_ref[...]
    gvb = gvn.astype(BF16)
    rows = []
    for j in range(tm // CHUNK):
        cols = []
        for g in range(N_GROUPS):
            sp = _dot(ws_ref[g], gvb[j * CHUNK:(j + 1) * CHUNK, g * gd:(g + 1) * gd]) + bs_ref[g]
            cols.append(sp)
        rows.append(jnp.concatenate(cols, axis=1))
    y_gmlp = gu * jnp.concatenate(rows, axis=0)
    n_ml = hm_ref.shape[1]
    mix = _dot(y_mlstm.astype(BF16), wout_ref[0:n_ml, :]) + _dot(y_gmlp.astype(BF16), wout_ref[n_ml:, :])
    x1 = x_ref[...] + mix

    hq = _rms(x1, gx_ref[...]).astype(BF16)
    q = _dot(hq, wq_ref[...])
    dh = kt_ref.shape[1]
    scale = dh ** -0.5
    heads = []
    for h in range(N_XHEADS):
        s = _dot(q[:, h * dh:(h + 1) * dh].astype(BF16), kt_ref[h]) * scale
        e = jnp.exp(s - jnp.max(s, axis=-1, keepdims=True))
        p = e / jnp.sum(e, axis=-1, keepdims=True)
        heads.append(_dot(p.astype(BF16), vm_ref[h]))
    att = jnp.concatenate(heads, axis=1).astype(BF16)
    x2 = x1 + _dot(att, wo_ref[...])
    x2_ref[...] = x2

    hn = _rms(x2, gmoe_ref[...])
    hn_ref[...] = hn
    lg = _dot(hn.astype(BF16), wr_ref[...]) + br_ref[...]
    lane = lax.broadcasted_iota(jnp.int32, lg.shape, 1).astype(F32)
    n_lanes = float(LANES)
    gmask = lane < N_EXPERT_GROUPS
    gmax = jnp.max(jnp.where(gmask, lg, -jnp.inf), axis=1, keepdims=True)
    ge = jnp.where(gmask, jnp.exp(lg - gmax), 0.0)
    p = ge / jnp.sum(ge, axis=1, keepdims=True)
    p_top = jnp.max(p, axis=1, keepdims=True)
    g_idx = jnp.min(jnp.where(jnp.where(gmask, p, -1.0) == p_top, lane, n_lanes), axis=1, keepdims=True)
    lo = N_EXPERT_GROUPS + EXPERTS_PER_GROUP * g_idx
    el = jnp.where(lane >= lo, jnp.where(lane < lo + EXPERTS_PER_GROUP, lg, -jnp.inf), -jnp.inf)
    v1 = jnp.max(el, axis=1, keepdims=True)
    i1 = jnp.min(jnp.where(el == v1, lane, n_lanes), axis=1, keepdims=True)
    el2 = jnp.where(lane == i1, -jnp.inf, el)
    v2 = jnp.max(el2, axis=1, keepdims=True)
    i2 = jnp.min(jnp.where(el2 == v2, lane, n_lanes), axis=1, keepdims=True)
    ex2 = jnp.exp(v2 - v1)
    den = 1.0 + ex2
    w1 = p_top * (1.0 / den)
    w2 = p_top * (ex2 / den)
    e1 = i1 - N_EXPERT_GROUPS
    e2 = i2 - N_EXPERT_GROUPS

    @pl.when(pl.program_id(0) == 0)
    def _():
        run_ref[...] = jnp.zeros_like(run_ref)

    is1 = lane == e1
    is2 = lane == e2
    oh = jnp.where(is1, 1.0, jnp.where(is2, 1.0, 0.0))
    before = _dot(ltri_ref[...], oh.astype(BF16)) + run_ref[0:1, :]
    rank1 = jnp.sum(jnp.where(is1, before, 0.0), axis=1, keepdims=True)
    rank2 = jnp.sum(jnp.where(is2, before, 0.0), axis=1, keepdims=True)
    total = run_ref[0:1, :] + jnp.sum(oh, axis=0, keepdims=True)
    run_ref[...] = jnp.broadcast_to(total, run_ref.shape)
    cnt_ref[...] = jnp.broadcast_to(total, cnt_ref.shape)
    rt = jnp.where(lane == 0.0, w1, jnp.where(lane == 1.0, w2, jnp.where(lane == 2.0, e1, jnp.where(
        lane == 3.0, e2, jnp.where(lane == 4.0, rank1, jnp.where(lane == 5.0, rank2, 0.0))))))
    rt_ref[...] = rt


def _mix_attn(x, hm, o, gu, gv, ln_g, ln_b, w_s, b_s, w_out, g_x, w_q, kt, vm, w_o, g_moe, w_r, b_r, S):
    T, D = x.shape
    tm = MIX_TILE
    per_b = S // tm
    row = lambda i: (i, 0)
    c2 = lambda i: (0, 0)
    c3 = lambda i: (0, 0, 0)
    full = lambda a: pl.BlockSpec(a.shape, c2 if a.ndim == 2 else c3)
    tile = lambda a: pl.BlockSpec((tm, a.shape[1]), row)
    ti = lax.broadcasted_iota(jnp.int32, (tm, tm), 0)
    si = lax.broadcasted_iota(jnp.int32, (tm, tm), 1)
    ltri = jnp.where(si < ti, 1.0, 0.0).astype(BF16)
    return pl.pallas_call(
        _mix_attn_kernel,
        grid=(T // tm,),
        in_specs=[
            tile(x), tile(hm), tile(o), tile(gu), tile(gv),
            full(ln_g), full(ln_b), full(w_s), full(b_s), full(w_out),
            full(g_x), full(w_q),
            pl.BlockSpec((None,) + kt.shape[1:], lambda i: (i // per_b, 0, 0, 0)),
            pl.BlockSpec((None,) + vm.shape[1:], lambda i: (i // per_b, 0, 0, 0)),
            full(w_o), full(g_moe), full(w_r), full(b_r), full(ltri),
        ],
        out_specs=[pl.BlockSpec((tm, D), row), pl.BlockSpec((tm, D), row), pl.BlockSpec((tm, LANES), row),
                   pl.BlockSpec((8, LANES), c2)],
        out_shape=[
            jax.ShapeDtypeStruct((T, D), F32),
            jax.ShapeDtypeStruct((T, D), F32),
            jax.ShapeDtypeStruct((T, LANES), F32),
            jax.ShapeDtypeStruct((8, LANES), F32),
        ],
        scratch_shapes=[pltpu.VMEM((8, LANES), F32)],
        compiler_params=pltpu.CompilerParams(
            dimension_semantics=("arbitrary",), vmem_limit_bytes=VMEM_LIMIT),
        name="mix_attn",
    )(x, hm, o, gu, gv, ln_g, ln_b, w_s, b_s, w_out, g_x, w_q, kt, vm, w_o, g_moe, w_r, b_r, ltri)


def _idx_copy(idx_hbm, block, idx_smem, slot, isem):
    return pltpu.make_async_copy(idx_hbm.at[block], idx_smem.at[slot], isem.at[slot])


def _dispatch_kernel(dd_hbm, h_ref, xz_hbm, xd_hbm, idx, sem, isem):
    del xz_hbm
    i = pl.program_id(0)
    nb = pl.num_programs(0)
    R = h_ref.shape[0]
    slot = lax.rem(i, 2)

    def row_copy(r, dst):
        return pltpu.make_async_copy(h_ref.at[pl.ds(r, 1), :], xd_hbm.at[pl.ds(dst, 1), :], sem)

    @pl.when(i == 0)
    def _():
        first = _idx_copy(dd_hbm, 0, idx, 0, isem)
        first.start()
        first.wait()

    @pl.when(i + 1 < nb)
    def _():
        _idx_copy(dd_hbm, i + 1, idx, 1 - slot, isem).start()

    for r in range(R):
        for k in range(TOP_K):
            row_copy(r, idx[slot, k * R + r]).start()
    for r in range(TOP_K * R):
        row_copy(0, 0).wait()

    @pl.when(i + 1 < nb)
    def _():
        _idx_copy(dd_hbm, i + 1, idx, 1 - slot, isem).wait()


def _dispatch(dd, hn, n_slots):
    T, D = hn.shape
    nt, two_r = dd.shape
    R = two_r // TOP_K
    any_spec = pl.BlockSpec(memory_space=pl.ANY)
    return pl.pallas_call(
        _dispatch_kernel,
        grid=(nt,),
        in_specs=[any_spec, pl.BlockSpec((R, D), lambda i: (i, 0)), any_spec],
        out_specs=any_spec,
        scratch_shapes=[pltpu.SMEM((2, two_r), jnp.int32),
                        pltpu.SemaphoreType.DMA(()), pltpu.SemaphoreType.DMA((2,))],
        out_shape=jax.ShapeDtypeStruct((n_slots, D), hn.dtype),
        input_output_aliases={2: 0},
        compiler_params=pltpu.CompilerParams(
            dimension_semantics=("arbitrary",), vmem_limit_bytes=VMEM_LIMIT),
        name="dispatch",
    )(dd, hn, jnp.zeros((n_slots, D), hn.dtype))


def _expert_kernel(blk_e_ref, n_used_ref, x_ref, wgu_ref, wd_ref, y_ref):
    used = pl.program_id(0) < n_used_ref[0]

    @pl.when(used)
    def _():
        xb = x_ref[...].astype(BF16)
        gu = _dot(xb, wgu_ref[...])
        de = wd_ref.shape[0]
        act = (_silu(gu[:, 0:de]) * gu[:, de:2 * de]).astype(BF16)
        y_ref[...] = _dot(act, wd_ref[...])

    @pl.when(jnp.logical_not(used))
    def _():
        y_ref[...] = jnp.zeros_like(y_ref)


def _experts(blk_e, n_used, x_disp, w_gu, w_d):
    P, D = x_disp.shape
    R = EXPERT_BLOCK
    nb = P // R
    de = w_d.shape[1]
    last = lambda i, nu: jnp.minimum(i, nu[0] - 1)
    return pl.pallas_call(
        _expert_kernel,
        grid_spec=pltpu.PrefetchScalarGridSpec(
            num_scalar_prefetch=2,
            grid=(nb,),
            in_specs=[
                pl.BlockSpec((R, D), lambda i, be, nu: (last(i, nu), 0)),
                pl.BlockSpec((None, D, 2 * de), lambda i, be, nu: (be[last(i, nu)], 0, 0)),
                pl.BlockSpec((None, de, D), lambda i, be, nu: (be[last(i, nu)], 0, 0)),
            ],
            out_specs=pl.BlockSpec((R, D), lambda i, be, nu: (i, 0)),
        ),
        out_shape=jax.ShapeDtypeStruct((P, D), F32),
        compiler_params=pltpu.CompilerParams(
            dimension_semantics=("arbitrary",), vmem_limit_bytes=VMEM_LIMIT),
        name="experts",
    )(blk_e, n_used, x_disp, w_gu, w_d)


def _combine_kernel(dd_hbm, y_hbm, x2_ref, w0_ref, w1_ref, gf_ref, out_ref, buf, idx, sem, isem):
    i = pl.program_id(0)
    nb = pl.num_programs(0)
    R = x2_ref.shape[0]
    slot = lax.rem(i, 2)

    def row_copy(src, s, r):
        return pltpu.make_async_copy(y_hbm.at[pl.ds(src, 1), :], buf.at[s, pl.ds(r, 1), :], sem.at[s])

    def issue_rows(s):
        for r in range(TOP_K * R):
            row_copy(idx[s, r], s, r).start()

    @pl.when(i == 0)
    def _():
        first = _idx_copy(dd_hbm, 0, idx, 0, isem)
        first.start()
        first.wait()
        issue_rows(0)

        @pl.when(nb > 1)
        def _():
            _idx_copy(dd_hbm, 1, idx, 1, isem).start()

    @pl.when(i + 1 < nb)
    def _():
        _idx_copy(dd_hbm, i + 1, idx, 1 - slot, isem).wait()
        issue_rows(1 - slot)

    @pl.when(i + 2 < nb)
    def _():
        _idx_copy(dd_hbm, i + 2, idx, slot, isem).start()

    for r in range(TOP_K * R):
        row_copy(0, slot, r).wait()

    moe = w0_ref[...] * buf[slot, 0:R, :] + w1_ref[...] * buf[slot, R:2 * R, :]
    out_ref[...] = _rms(x2_ref[...] + moe, gf_ref[...])


def _combine(dd, y_disp, x2, w0, w1, g_final):
    T, D = x2.shape
    R = COMBINE_TILE
    row = lambda i: (i, 0)
    return pl.pallas_call(
        _combine_kernel,
        grid=(T // R,),
        in_specs=[
            pl.BlockSpec(memory_space=pl.ANY),
            pl.BlockSpec(memory_space=pl.ANY),
            pl.BlockSpec((R, D), row),
            pl.BlockSpec((R, 1), row),
            pl.BlockSpec((R, 1), row),
            pl.BlockSpec((1, D), lambda i: (0, 0)),
        ],
        out_specs=pl.BlockSpec((R, D), row),
        scratch_shapes=[pltpu.VMEM((2, TOP_K * R, D), F32), pltpu.SMEM((2, TOP_K * R), jnp.int32),
                        pltpu.SemaphoreType.DMA((2,)), pltpu.SemaphoreType.DMA((2,))],
        out_shape=jax.ShapeDtypeStruct((T, D), F32),
        compiler_params=pltpu.CompilerParams(
            dimension_semantics=("arbitrary",), vmem_limit_bytes=VMEM_LIMIT),
        name="combine",
    )(dd, y_disp, x2, w0, w1, g_final)


def _route_tables(route, counts_row, T):
    R = EXPERT_BLOCK
    counts = counts_row[0, 0:N_EXPERTS].astype(jnp.int32)
    padded = ((counts + R - 1) // R) * R
    pend = jnp.cumsum(padded)
    pstart = pend - padded
    nb = (T * TOP_K) // R + N_EXPERTS
    first_slot = jnp.arange(nb, dtype=jnp.int32) * R
    blk_e = jnp.sum((pend[None, :] <= first_slot[:, None]).astype(jnp.int32), axis=1)
    blk_e = jnp.minimum(blk_e, N_EXPERTS - 1)
    n_used = (pend[N_EXPERTS - 1:N_EXPERTS] // R).astype(jnp.int32)
    e = route[:, 2:4].astype(jnp.int32)
    rank = route[:, 4:6].astype(jnp.int32)
    sel = e[:, :, None] == jnp.arange(N_EXPERTS, dtype=jnp.int32)[None, None, :]
    dest = jnp.sum(jnp.where(sel, pstart[None, None, :], 0), axis=2) + rank
    d = dest.reshape(T // COMBINE_TILE, COMBINE_TILE, TOP_K)
    dd = jnp.transpose(d, (0, 2, 1)).reshape(T // COMBINE_TILE, TOP_K * COMBINE_TILE)
    return blk_e, n_used, dd, nb * R


def kernel(x, mem, g_mix, w_in, conv_w, conv_b, gate_b, g_head, ln_v_g, ln_v_b, w_s, b_s, w_out,
           g_xattn, g_mem, w_q_x, w_kv_x, w_o_x, g_moe, w_rg, b_rg, w_re, b_re,
           w_gate, w_up, w_down, g_final):
    B, S, D = x.shape
    T = B * S
    NC = S // CHUNK
    d_ml = N_HEADS * HEAD_DIM
    n_gates = 4 * N_HEADS
    assert w_in.shape[0] == 1, "one layer"
    assert S % PROJ_TILE == 0 and S % MIX_TILE == 0 and T % COMBINE_TILE == 0
    assert (T * TOP_K) % EXPERT_BLOCK == 0

    w = w_in[0]
    c_g = 4 * d_ml
    w_main = jnp.concatenate([w[:, 0:c_g], w[:, c_g + n_gates:]], axis=1).astype(BF16)
    wg = w[:, c_g:c_g + n_gates].reshape(D, 4, N_HEADS)
    wg = jnp.pad(jnp.transpose(wg, (2, 1, 0)), ((0, 0), (0, 4), (0, 0)))
    w_gt = wg.reshape(N_HEADS * 8, D).astype(BF16)
    gb = jnp.pad(jnp.transpose(gate_b[0].reshape(4, N_HEADS), (1, 0)), ((0, 0), (0, 4)))
    gb = gb.reshape(N_HEADS * 8, 1).astype(F32)

    qk, v, o, gu, gv, gt = _proj_in(x, g_mix[0:1], w_main, w_gt, gb)

    h_ml = _mlstm(qk.reshape(B, S, 2 * d_ml), v.reshape(B, NC, CHUNK, d_ml), gt,
                  conv_w[0].reshape(CONV_WIDTH, 2 * d_ml), conv_b[0:1], g_head[0:1])
    h_ml = h_ml.reshape(T, d_ml)

    dh = D // N_XHEADS
    w_kv = w_kv_x[0]
    w_kt = jnp.transpose(w_kv[:, 0:D].reshape(D, N_XHEADS, dh), (1, 2, 0)).astype(BF16)
    w_v = jnp.transpose(w_kv[:, D:2 * D].reshape(D, N_XHEADS, dh), (1, 0, 2)).astype(BF16)
    kt, vm = _kv_proj(mem, g_mem[0:1], w_kt, w_v)

    w_r = jnp.concatenate([w_rg[0], w_re[0]], axis=1)
    w_r = jnp.pad(w_r, ((0, 0), (0, LANES - w_r.shape[1]))).astype(BF16)
    b_r = jnp.concatenate([b_rg[0], b_re[0]])
    b_r = jnp.pad(b_r, (0, LANES - b_r.shape[0])).reshape(1, LANES).astype(F32)
    bs_b = jnp.broadcast_to(b_s[0][:, :, None], (N_GROUPS, CHUNK, HEAD_DIM)).astype(F32)

    x2, hn, route, counts = _mix_attn(
        x.reshape(T, D), h_ml, o, gu, gv, ln_v_g[0:1], ln_v_b[0:1], w_s[0].astype(BF16), bs_b,
        w_out[0].astype(BF16), g_xattn[0:1], w_q_x[0].astype(BF16), kt, vm, w_o_x[0].astype(BF16),
        g_moe[0:1], w_r, b_r, S)

    blk_e, n_used, dd, n_slots = _route_tables(route, counts, T)
    x_disp = _dispatch(dd, hn, n_slots)

    w_gu = jnp.concatenate([w_gate[0], w_up[0]], axis=2).astype(BF16)
    y_disp = _experts(blk_e, n_used, x_disp, w_gu, w_down[0].astype(BF16))

    out = _combine(dd, y_disp, x2, route[:, 0:1], route[:, 1:2], g_final.reshape(1, D))
    return out.reshape(B, S, D)
```

```python
import jax
import jax.numpy as jnp
from jax import lax
from jax.experimental import pallas as pl
from jax.experimental.pallas import tpu as pltpu

F32 = jnp.float32
BF16 = jnp.bfloat16

RMS_EPS = 1e-6
LN_EPS = 1e-5
NEG_INIT = -1e30

N_HEADS = 4
HEAD_DIM = 128
CHUNK = 128
CONV_WIDTH = 5
HALO = 8
N_GROUPS = 4
N_XHEADS = 4
N_EXPERT_GROUPS = 4
EXPERTS_PER_GROUP = 8
N_EXPERTS = N_EXPERT_GROUPS * EXPERTS_PER_GROUP
TOP_K = 2
LANES = 128

PROJ_TILE = 512
MIX_TILE = 512
MIX_SUB = 512
EXPERT_BLOCK = 256
COMBINE_TILE = 256
CONV_ROWS = 64
HEADS_PER_STEP = 2
MLSTM_UNROLL = 4
VMEM_LIMIT = 56 * 1024 * 1024


def _dot(a, b):
    return jnp.dot(a, b, preferred_element_type=F32)


def _dot_nt(a, b):
    return lax.dot_general(a, b, (((1,), (1,)), ((), ())), preferred_element_type=F32)


def _rms(x, g):
    return x * lax.rsqrt(jnp.mean(x * x, axis=-1, keepdims=True) + RMS_EPS) * g


def _sigmoid(x):
    return 1.0 / (1.0 + jnp.exp(-x))


def _silu(x):
    return x * _sigmoid(x)


def _gelu_tanh(x):
    c = 0.7978845608028654
    return 0.5 * x * (1.0 + jnp.tanh(c * (x + 0.044715 * (x * x * x))))


def _log_sigmoid(x):
    return jnp.minimum(x, 0.0) - jnp.log(1.0 + jnp.exp(-jnp.abs(x)))


def _proj_in_kernel(x_ref, g_ref, w_ref, wgt_ref, gb_ref,
                    qk_ref, v_ref, o_ref, gu_ref, gv_ref, gt_ref):
    xb = _rms(x_ref[...], g_ref[...]).astype(BF16)
    qk_ref[...] = _dot(xb, w_ref[:, 0:1024])
    v_ref[...] = _dot(xb, w_ref[:, 1024:1536]).astype(BF16)
    o_ref[...] = _dot(xb, w_ref[:, 1536:2048])
    gu_ref[...] = _dot(xb, w_ref[:, 2048:2560])
    gv_ref[...] = _dot(xb, w_ref[:, 2560:3072])
    gt = _dot_nt(wgt_ref[...], xb) + gb_ref[...]
    for h in range(N_HEADS):
        for j in range(gt_ref.shape[1]):
            gt_ref[h, j] = gt[8 * h:8 * h + 8, CHUNK * j:CHUNK * (j + 1)]


def _proj_in(x, g_mix, w_main, w_gt, gate_b):
    B, S, D = x.shape
    tm = PROJ_TILE
    nj = tm // CHUNK
    NC = S // CHUNK
    T = B * S
    grid = (B, S // tm)
    row = lambda b, i: (b * (S // tm) + i, 0)
    const = lambda b, i: (0, 0)
    outs = pl.pallas_call(
        _proj_in_kernel,
        grid=grid,
        in_specs=[
            pl.BlockSpec((None, tm, D), lambda b, i: (b, i, 0)),
            pl.BlockSpec((1, D), const),
            pl.BlockSpec(w_main.shape, const),
            pl.BlockSpec(w_gt.shape, const),
            pl.BlockSpec(gate_b.shape, const),
        ],
        out_specs=[
            pl.BlockSpec((tm, 1024), row),
            pl.BlockSpec((tm, 512), row),
            pl.BlockSpec((tm, 512), row),
            pl.BlockSpec((tm, 512), row),
            pl.BlockSpec((tm, 512), row),
            pl.BlockSpec((None, N_HEADS, nj, 8, CHUNK), lambda b, i: (b, 0, i, 0, 0)),
        ],
        out_shape=[
            jax.ShapeDtypeStruct((T, 1024), F32),
            jax.ShapeDtypeStruct((T, 512), BF16),
            jax.ShapeDtypeStruct((T, 512), F32),
            jax.ShapeDtypeStruct((T, 512), F32),
            jax.ShapeDtypeStruct((T, 512), F32),
            jax.ShapeDtypeStruct((B, N_HEADS, NC, 8, CHUNK), F32),
        ],
        compiler_params=pltpu.CompilerParams(
            dimension_semantics=("parallel", "parallel"), vmem_limit_bytes=VMEM_LIMIT),
        name="proj_in",
    )(x, g_mix, w_main, w_gt, gate_b)
    return outs


def _conv_silu(src_ref, w, bias, r0, n_rows):
    S = src_ref.shape[0]
    acc = None
    for j in range(CONV_WIDTH):
        d = j - CONV_WIDTH // 2
        if r0 + d < 0 or r0 + d + n_rows > S:
            blk = src_ref[r0:r0 + n_rows, :]
            ridx = lax.broadcasted_iota(jnp.int32, blk.shape, 0)
            sh = pltpu.roll(blk, (-d) % n_rows, axis=0)
            tap = jnp.where((ridx + d >= 0) & (ridx + d < n_rows), sh, 0.0)
        else:
            tap = src_ref[r0 + d:r0 + d + n_rows, :]
        term = w[j:j + 1, :] * tap
        acc = term if acc is None else acc + term
    return _silu(acc + bias)


G_B, G_R, G_CM, G_WK, G_F, G_ML = range(6)


def _gate_rows(g, t_idx, s_idx):
    n = g.shape[0]
    ls = _log_sigmoid(g)
    hi = ls.astype(BF16).astype(F32)
    r1 = ls - hi
    mid = r1.astype(BF16).astype(F32)
    lo = (r1 - mid).astype(BF16).astype(F32)
    parts = jnp.concatenate([hi, mid, lo], axis=0).astype(BF16)
    one = lambda m: jnp.where(m, 1.0, 0.0).astype(BF16)
    rhs = jnp.concatenate([one(t_idx <= s_idx), one(t_idx >= s_idx), jnp.ones((CHUNK, CHUNK), BF16)], axis=1)
    cs = _dot(parts, rhs)
    cum = cs[0:n] + cs[n:2 * n] + cs[2 * n:3 * n]
    up1 = lambda a: pltpu.roll(a, n - 1, axis=0)
    fwd_row = lax.rem(lax.broadcasted_iota(jnp.int32, (n, CHUNK), 0), 8) == 0
    lane = lax.broadcasted_iota(jnp.int32, (n, CHUNK), 1)
    b = jnp.where(fwd_row, up1(cum[:, 0:CHUNK]), up1(cum[:, CHUNK:2 * CHUNK]))
    r = g - b
    f = up1(cum[:, 2 * CHUNK:3 * CHUNK])
    a = f + r
    ml = jnp.broadcast_to(jnp.max(a, axis=1, keepdims=True), a.shape)
    wk = jnp.exp(a - ml)
    pm = r
    sm = r
    s = 1
    while s < CHUNK:
        pm = jnp.maximum(pm, jnp.where(lane >= s, pltpu.roll(pm, s, axis=1), -jnp.inf))
        sm = jnp.maximum(sm, jnp.where(lane < CHUNK - s, pltpu.roll(sm, CHUNK - s, axis=1), -jnp.inf))
        s *= 2
    cm = jnp.where(fwd_row, pm, sm)
    return b, r, cm, wk, f, ml


def _mlstm_kernel(q_ref, k_ref, v_ref, g_ref, cwq_ref, cwk_ref, cbq_ref, cbk_ref, gh_ref,
                  out_ref, qc_ref, kt_ref, gate_ref, caug_ref, m_ref):
    NC = v_ref.shape[0]
    hps = HEADS_PER_STEP
    k_scale = HEAD_DIM ** -0.5

    cwq, cwk = cwq_ref[...], cwk_ref[...]
    cbq, cbk = cbq_ref[...], cbk_ref[...]
    n_sub = CHUNK // CONV_ROWS
    for c in range(NC):
        qb = jnp.concatenate([_conv_silu(q_ref, cwq, cbq, c * CHUNK + i * CONV_ROWS, CONV_ROWS)
                              for i in range(n_sub)], axis=0)
        kb = jnp.concatenate([_conv_silu(k_ref, cwk, cbk, c * CHUNK + i * CONV_ROWS, CONV_ROWS)
                              for i in range(n_sub)], axis=0) * k_scale
        for hd in range(hps):
            cols = slice(hd * HEAD_DIM, (hd + 1) * HEAD_DIM)
            qc_ref[hd, c] = qb[:, cols].astype(BF16)
            kt_ref[hd, c] = kb[:, cols].T

    t_idx = lax.broadcasted_iota(jnp.int32, (CHUNK, CHUNK), 0)
    s_idx = lax.broadcasted_iota(jnp.int32, (CHUNK, CHUNK), 1)
    ones = jnp.ones((CHUNK, HEAD_DIM), BF16)
    gh = gh_ref[...]

    for hd in range(hps):
        rows = _gate_rows(g_ref[hd].reshape(NC * 8, CHUNK), t_idx, s_idx)
        for kind, val in enumerate(rows):
            gate_ref[kind, hd] = val.reshape(NC, 8, CHUNK)

    def col_form(row):
        return jnp.broadcast_to(row, (CHUNK, CHUNK)).T

    def chunk_step(hd, c, direction):
        st = 2 * hd + direction
        row = slice(2 * direction, 2 * direction + 1)
        b_r = gate_ref[G_B, hd, c][row]
        r_r = gate_ref[G_R, hd, c][row]
        cm_r = gate_ref[G_CM, hd, c][row]
        wk_r = gate_ref[G_WK, hd, c][row]
        f_tot = gate_ref[G_F, hd, c][row]
        m_loc = gate_ref[G_ML, hd, c][row]
        causal = (s_idx <= t_idx) if direction == 0 else (s_idx >= t_idx)

        m_prev = m_ref[st][0:1]
        caug = caug_ref[st]
        q = qc_ref[hd, c]
        kt = kt_ref[hd, c]
        vaug = jnp.concatenate([v_ref[c, :, hd * HEAD_DIM:(hd + 1) * HEAD_DIM], ones], axis=1)

        b_c = col_form(b_r)
        mx_c = jnp.maximum(col_form(cm_r), m_prev)
        w_mat = jnp.where(causal, jnp.exp(r_r - mx_c), 0.0)
        s_inter = jnp.exp(m_prev - mx_c)
        scores = _dot(q, kt.astype(BF16))
        wqk = (w_mat * scores).astype(BF16)
        intra = _dot(wqk, vaug)
        carried = _dot(q, caug.astype(BF16))
        num = intra[:, 0:HEAD_DIM] + s_inter * carried[:, 0:HEAD_DIM]
        den = intra[:, HEAD_DIM:2 * HEAD_DIM] + s_inter * carried[:, HEAD_DIM:2 * HEAD_DIM]
        h = num / jnp.maximum(jnp.abs(den), jnp.exp(-(b_c + mx_c)))

        kw = (kt * wk_r).astype(BF16)
        c_loc = _dot(kw, vaug)
        m_new = jnp.maximum(f_tot + m_prev, m_loc)
        s_old = jnp.exp(f_tot + m_prev - m_new)
        s_new = jnp.exp(m_loc - m_new)
        wide = lambda a: jnp.concatenate([a, a], axis=1)
        caug_ref[st] = wide(s_old) * caug + wide(s_new) * c_loc
        m_ref[st] = jnp.broadcast_to(m_new, m_ref.shape[1:])
        return h

    caug_ref[...] = jnp.zeros_like(caug_ref)
    m_ref[...] = jnp.full(m_ref.shape, NEG_INIT, F32)

    def finish(hd, hs):
        cols = slice(hd * HEAD_DIM, (hd + 1) * HEAD_DIM)
        return hs * lax.rsqrt(jnp.mean(hs * hs, axis=-1, keepdims=True) + RMS_EPS) * gh[:, cols]

    def make_body(final):
        def body(i, carry):
            cf = i
            cb = NC - 1 - i
            for hd in range(hps):
                cols = slice(hd * HEAD_DIM, (hd + 1) * HEAD_DIM)
                h_f = chunk_step(hd, cf, 0)
                h_b = chunk_step(hd, cb, 1)
                if final:
                    out_ref[cf, :, cols] = finish(hd, out_ref[cf, :, cols] + h_f)
                    out_ref[cb, :, cols] = finish(hd, out_ref[cb, :, cols] + h_b)
                else:
                    out_ref[cf, :, cols] = h_f
                    out_ref[cb, :, cols] = h_b
            return carry
        return body

    lax.fori_loop(0, NC // 2, make_body(False), 0, unroll=MLSTM_UNROLL)
    lax.fori_loop(NC // 2, NC, make_body(True), 0, unroll=MLSTM_UNROLL)


def _mlstm(qk, v, gt, conv_w, conv_b, g_head):
    B, S = qk.shape[0], qk.shape[1]
    NC = S // CHUNK
    assert NC % 2 == 0
    hps = HEADS_PER_STEP
    w = hps * HEAD_DIM
    k_off = N_HEADS // hps
    seq = lambda off: pl.BlockSpec((None, S, w), lambda b, p: (b, 0, p + off))
    blk = lambda off: pl.BlockSpec((None, NC, CHUNK, w), lambda b, p: (b, 0, 0, p + off))
    par = lambda r, off: pl.BlockSpec((r, w), lambda b, p: (0, p + off))
    return pl.pallas_call(
        _mlstm_kernel,
        grid=(B, N_HEADS // hps),
        in_specs=[
            seq(0), seq(k_off), blk(0),
            pl.BlockSpec((None, hps, NC, 8, CHUNK), lambda b, p: (b, p, 0, 0, 0)),
            par(CONV_WIDTH, 0), par(CONV_WIDTH, k_off), par(1, 0), par(1, k_off), par(1, 0),
        ],
        out_specs=blk(0),
        out_shape=jax.ShapeDtypeStruct(v.shape, F32),
        scratch_shapes=[
            pltpu.VMEM((hps, NC, CHUNK, HEAD_DIM), BF16),
            pltpu.VMEM((hps, NC, HEAD_DIM, CHUNK), F32),
            pltpu.VMEM((6, hps, NC, 8, CHUNK), F32),
            pltpu.VMEM((2 * hps, HEAD_DIM, 2 * HEAD_DIM), F32),
            pltpu.VMEM((2 * hps, 8, LANES), F32),
        ],
        compiler_params=pltpu.CompilerParams(
            dimension_semantics=("parallel", "parallel"), vmem_limit_bytes=VMEM_LIMIT),
        name="mlstm",
    )(qk, qk, v, gt, conv_w, conv_w, conv_b, conv_b, g_head)


def _kv_kernel(mem_ref, g_ref, wkt_ref, wv_ref, kt_ref, v_ref):
    mn = _rms(mem_ref[...], g_ref[...]).astype(BF16)
    for h in range(N_XHEADS):
        kt_ref[h] = _dot_nt(wkt_ref[h], mn).astype(BF16)
        v_ref[h] = _dot(mn, wv_ref[h]).astype(BF16)


def _kv_proj(mem, g_mem, w_kt, w_v):
    B, M, D = mem.shape
    dh = D // N_XHEADS
    return pl.pallas_call(
        _kv_kernel,
        grid=(B,),
        in_specs=[
            pl.BlockSpec((None, M, D), lambda b: (b, 0, 0)),
            pl.BlockSpec((1, D), lambda b: (0, 0)),
            pl.BlockSpec(w_kt.shape, lambda b: (0, 0, 0)),
            pl.BlockSpec(w_v.shape, lambda b: (0, 0, 0)),
        ],
        out_specs=[
            pl.BlockSpec((None, N_XHEADS, dh, M), lambda b: (b, 0, 0, 0)),
            pl.BlockSpec((None, N_XHEADS, M, dh), lambda b: (b, 0, 0, 0)),
        ],
        out_shape=[
            jax.ShapeDtypeStruct((B, N_XHEADS, dh, M), BF16),
            jax.ShapeDtypeStruct((B, N_XHEADS, M, dh), BF16),
        ],
        compiler_params=pltpu.CompilerParams(
            dimension_semantics=("parallel",), vmem_limit_bytes=VMEM_LIMIT),
        name="kv_proj",
    )(mem, g_mem, w_kt, w_v)


def _mix_rows(r0, n, x_ref, hm_ref, o_ref, gu_ref, gv_ref, lng_ref, lnb_ref, ws_ref, bs_ref, wout_ref,
              gx_ref, wq_ref, kt_ref, vm_ref, wo_ref, gmoe_ref, wr_ref, br_ref, x2_ref, hn_ref):
    rs = slice(r0, r0 + n)
    gd = HEAD_DIM
    y_mlstm = _sigmoid(o_ref[rs, :]) * hm_ref[rs, :]
    gu = _gelu_tanh(gu_ref[rs, :])
    gv = _gelu_tanh(gv_ref[rs, :])
    mu = jnp.mean(gv, axis=-1, keepdims=True)
    gc = gv - mu
    gvn = gc * lax.rsqrt(jnp.mean(gc * gc, axis=-1, keepdims=True) + LN_EPS) * lng_ref[...] + lnb_ref[...]
    gvb = gvn.astype(BF16)
    rows = []
    for j in range(n // CHUNK):
        cols = []
        for g in range(N_GROUPS):
            sp = _dot(ws_ref[g], gvb[j * CHUNK:(j + 1) * CHUNK, g * gd:(g + 1) * gd]) + bs_ref[g]
            cols.append(sp)
        rows.append(jnp.concatenate(cols, axis=1))
    y_gmlp = gu * jnp.concatenate(rows, axis=0)
    n_ml = hm_ref.shape[1]
    mix = _dot(y_mlstm.astype(BF16), wout_ref[0:n_ml, :]) + _dot(y_gmlp.astype(BF16), wout_ref[n_ml:, :])
    x1 = x_ref[rs, :] + mix

    hq = _rms(x1, gx_ref[...]).astype(BF16)
    q = _dot(hq, wq_ref[...])
    dh = kt_ref.shape[1]
    scale = dh ** -0.5
    heads = []
    for h in range(N_XHEADS):
        s = _dot(q[:, h * dh:(h + 1) * dh].astype(BF16), kt_ref[h]) * scale
        e = jnp.exp(s - jnp.max(s, axis=-1, keepdims=True))
        p = e / jnp.sum(e, axis=-1, keepdims=True)
        heads.append(_dot(p.astype(BF16), vm_ref[h]))
    att = jnp.concatenate(heads, axis=1).astype(BF16)
    x2 = x1 + _dot(att, wo_ref[...])
    x2_ref[rs, :] = x2

    hn = _rms(x2, gmoe_ref[...])
    hn_ref[rs, :] = hn
    return _dot(hn.astype(BF16), wr_ref[...]) + br_ref[...]


def _mix_attn_kernel(x_ref, hm_ref, o_ref, gu_ref, gv_ref, lng_ref, lnb_ref, ws_ref, bs_ref, wout_ref,
                     gx_ref, wq_ref, kt_ref, vm_ref, wo_ref, gmoe_ref, wr_ref, br_ref, ltri_ref,
                     x2_ref, hn_ref, rt_ref, cnt_ref, run_ref):
    tm = x_ref.shape[0]
    parts = [_mix_rows(r0, MIX_SUB, x_ref, hm_ref, o_ref, gu_ref, gv_ref, lng_ref, lnb_ref, ws_ref, bs_ref,
                       wout_ref, gx_ref, wq_ref, kt_ref, vm_ref, wo_ref, gmoe_ref, wr_ref, br_ref,
                       x2_ref, hn_ref)
             for r0 in range(0, tm, MIX_SUB)]
    lg = jnp.concatenate(parts, axis=0)

    lane = lax.broadcasted_iota(jnp.int32, lg.shape, 1).astype(F32)
    n_lanes = float(LANES)
    gmask = lane < N_EXPERT_GROUPS
    gmax = jnp.max(jnp.where(gmask, lg, -jnp.inf), axis=1, keepdims=True)
    ge = jnp.where(gmask, jnp.exp(lg - gmax), 0.0)
    p = ge / jnp.sum(ge, axis=1, keepdims=True)
    p_top = jnp.max(p, axis=1, keepdims=True)
    g_idx = jnp.min(jnp.where(jnp.where(gmask, p, -1.0) == p_top, lane, n_lanes), axis=1, keepdims=True)
    lo = N_EXPERT_GROUPS + EXPERTS_PER_GROUP * g_idx
    el = jnp.where(lane >= lo, jnp.where(lane < lo + EXPERTS_PER_GROUP, lg, -jnp.inf), -jnp.inf)
    v1 = jnp.max(el, axis=1, keepdims=True)
    i1 = jnp.min(jnp.where(el == v1, lane, n_lanes), axis=1, keepdims=True)
    el2 = jnp.where(lane == i1, -jnp.inf, el)
    v2 = jnp.max(el2, axis=1, keepdims=True)
    i2 = jnp.min(jnp.where(el2 == v2, lane, n_lanes), axis=1, keepdims=True)
    ex2 = jnp.exp(v2 - v1)
    den = 1.0 + ex2
    w1 = p_top * (1.0 / den)
    w2 = p_top * (ex2 / den)
    e1 = i1 - N_EXPERT_GROUPS
    e2 = i2 - N_EXPERT_GROUPS

    @pl.when(pl.program_id(0) == 0)
    def _():
        run_ref[...] = jnp.zeros_like(run_ref)

    is1 = lane == e1
    is2 = lane == e2
    oh = jnp.where(is1, 1.0, jnp.where(is2, 1.0, 0.0))
    before = _dot(ltri_ref[...], oh.astype(BF16)) + run_ref[0:1, :]
    rank1 = jnp.sum(jnp.where(is1, before, 0.0), axis=1, keepdims=True)
    rank2 = jnp.sum(jnp.where(is2, before, 0.0), axis=1, keepdims=True)
    total = run_ref[0:1, :] + jnp.sum(oh, axis=0, keepdims=True)
    run_ref[...] = jnp.broadcast_to(total, run_ref.shape)
    cnt_ref[...] = jnp.broadcast_to(total, cnt_ref.shape)
    rt = jnp.where(lane == 0.0, w1, jnp.where(lane == 1.0, w2, jnp.where(lane == 2.0, e1, jnp.where(
        lane == 3.0, e2, jnp.where(lane == 4.0, rank1, jnp.where(lane == 5.0, rank2, 0.0))))))
    rt_ref[...] = rt


def _mix_attn(x, hm, o, gu, gv, ln_g, ln_b, w_s, b_s, w_out, g_x, w_q, kt, vm, w_o, g_moe, w_r, b_r, S):
    T, D = x.shape
    tm = MIX_TILE
    per_b = S // tm
    row = lambda i: (i, 0)
    c2 = lambda i: (0, 0)
    c3 = lambda i: (0, 0, 0)
    full = lambda a: pl.BlockSpec(a.shape, c2 if a.ndim == 2 else c3)
    tile = lambda a: pl.BlockSpec((tm, a.shape[1]), row)
    ti = lax.broadcasted_iota(jnp.int32, (tm, tm), 0)
    si = lax.broadcasted_iota(jnp.int32, (tm, tm), 1)
    ltri = jnp.where(si < ti, 1.0, 0.0).astype(BF16)
    return pl.pallas_call(
        _mix_attn_kernel,
        grid=(T // tm,),
        in_specs=[
            tile(x), tile(hm), tile(o), tile(gu), tile(gv),
            full(ln_g), full(ln_b), full(w_s), full(b_s), full(w_out),
            full(g_x), full(w_q),
            pl.BlockSpec((None,) + kt.shape[1:], lambda i: (i // per_b, 0, 0, 0)),
            pl.BlockSpec((None,) + vm.shape[1:], lambda i: (i // per_b, 0, 0, 0)),
            full(w_o), full(g_moe), full(w_r), full(b_r), full(ltri),
        ],
        out_specs=[pl.BlockSpec((tm, D), row), pl.BlockSpec((tm, D), row), pl.BlockSpec((tm, LANES), row),
                   pl.BlockSpec((8, LANES), c2)],
        out_shape=[
            jax.ShapeDtypeStruct((T, D), F32),
            jax.ShapeDtypeStruct((T, D), F32),
            jax.ShapeDtypeStruct((T, LANES), F32),
            jax.ShapeDtypeStruct((8, LANES), F32),
        ],
        scratch_shapes=[pltpu.VMEM((8, LANES), F32)],
        compiler_params=pltpu.CompilerParams(
            dimension_semantics=("arbitrary",), vmem_limit_bytes=VMEM_LIMIT),
        name="mix_attn",
    )(x, hm, o, gu, gv, ln_g, ln_b, w_s, b_s, w_out, g_x, w_q, kt, vm, w_o, g_moe, w_r, b_r, ltri)


def _idx_copy(idx_hbm, block, idx_smem, slot, isem):
    return pltpu.make_async_copy(idx_hbm.at[block], idx_smem.at[slot], isem.at[slot])


def _dispatch_kernel(pad_start_ref, pad_n_ref, dd_hbm, h_ref, xd_hbm, idx, zrow, sem, isem, zsem):
    i = pl.program_id(0)
    nb = pl.num_programs(0)
    R = h_ref.shape[0]
    slot = lax.rem(i, 2)

    def row_copy(r, dst):
        return pltpu.make_async_copy(h_ref.at[pl.ds(r, 1), :], xd_hbm.at[pl.ds(dst, 1), :], sem)

    def zero_copy(dst):
        return pltpu.make_async_copy(zrow.at[pl.ds(0, 1), :], xd_hbm.at[pl.ds(dst, 1), :], zsem)

    @pl.when(i == 0)
    def _():
        first = _idx_copy(dd_hbm, 0, idx, 0, isem)
        first.start()
        zrow[...] = jnp.zeros_like(zrow)
        for e in range(pad_start_ref.shape[0]):
            base = pad_start_ref[e]
            count = pad_n_ref[e]

            def start_body(j, carry, base=base):
                zero_copy(base + j).start()
                return carry

            def wait_body(j, carry):
                zero_copy(0).wait()
                return carry

            lax.fori_loop(0, count, start_body, 0)
            lax.fori_loop(0, count, wait_body, 0)
        first.wait()

    @pl.when(i + 1 < nb)
    def _():
        _idx_copy(dd_hbm, i + 1, idx, 1 - slot, isem).start()

    for r in range(R):
        for k in range(TOP_K):
            row_copy(r, idx[slot, k * R + r]).start(priority=k)
    for r in range(TOP_K * R):
        row_copy(0, 0).wait()

    @pl.when(i + 1 < nb)
    def _():
        _idx_copy(dd_hbm, i + 1, idx, 1 - slot, isem).wait()


def _dispatch(pad_start, pad_n, dd, hn, n_slots):
    T, D = hn.shape
    nt, two_r = dd.shape
    R = two_r // TOP_K
    any_spec = pl.BlockSpec(memory_space=pl.ANY)
    return pl.pallas_call(
        _dispatch_kernel,
        grid_spec=pltpu.PrefetchScalarGridSpec(
            num_scalar_prefetch=2,
            grid=(nt,),
            in_specs=[any_spec, pl.BlockSpec((R, D), lambda i, ps, pn: (i, 0))],
            out_specs=any_spec,
            scratch_shapes=[pltpu.SMEM((2, two_r), jnp.int32), pltpu.VMEM((8, D), hn.dtype),
                            pltpu.SemaphoreType.DMA(()), pltpu.SemaphoreType.DMA((2,)),
                            pltpu.SemaphoreType.DMA(())],
        ),
        out_shape=jax.ShapeDtypeStruct((n_slots, D), hn.dtype),
        compiler_params=pltpu.CompilerParams(
            dimension_semantics=("arbitrary",), vmem_limit_bytes=VMEM_LIMIT),
        name="dispatch",
    )(pad_start, pad_n, dd, hn)


def _expert_kernel(blk_e_ref, n_used_ref, x_ref, wgu_ref, wd_ref, y_ref):
    used = pl.program_id(0) < n_used_ref[0]

    @pl.when(used)
    def _():
        xb = x_ref[...].astype(BF16)
        gu = _dot(xb, wgu_ref[...])
        de = wd_ref.shape[0]
        act = (_silu(gu[:, 0:de]) * gu[:, de:2 * de]).astype(BF16)
        y_ref[...] = _dot(act, wd_ref[...])

    @pl.when(jnp.logical_not(used))
    def _():
        y_ref[...] = jnp.zeros_like(y_ref)


def _experts(blk_e, n_used, x_disp, w_gu, w_d):
    P, D = x_disp.shape
    R = EXPERT_BLOCK
    nb = P // R
    de = w_d.shape[1]
    last = lambda i, nu: jnp.minimum(i, nu[0] - 1)
    return pl.pallas_call(
        _expert_kernel,
        grid_spec=pltpu.PrefetchScalarGridSpec(
            num_scalar_prefetch=2,
            grid=(nb,),
            in_specs=[
                pl.BlockSpec((R, D), lambda i, be, nu: (last(i, nu), 0)),
                pl.BlockSpec((None, D, 2 * de), lambda i, be, nu: (be[last(i, nu)], 0, 0)),
                pl.BlockSpec((None, de, D), lambda i, be, nu: (be[last(i, nu)], 0, 0)),
            ],
            out_specs=pl.BlockSpec((R, D), lambda i, be, nu: (i, 0)),
        ),
        out_shape=jax.ShapeDtypeStruct((P, D), F32),
        compiler_params=pltpu.CompilerParams(
            dimension_semantics=("arbitrary",), vmem_limit_bytes=VMEM_LIMIT),
        name="experts",
    )(blk_e, n_used, x_disp, w_gu, w_d)


def _combine_kernel(dd_hbm, y_hbm, x2_ref, w0_ref, w1_ref, gf_ref, out_ref, buf, idx, sem, isem):
    i = pl.program_id(0)
    nb = pl.num_programs(0)
    R = x2_ref.shape[0]
    slot = lax.rem(i, 2)

    def row_copy(src, s, r):
        return pltpu.make_async_copy(y_hbm.at[pl.ds(src, 1), :], buf.at[s, pl.ds(r, 1), :], sem.at[s])

    def issue_rows(s):
        for r in range(TOP_K * R):
            row_copy(idx[s, r], s, r).start(priority=r % 2)

    @pl.when(i == 0)
    def _():
        first = _idx_copy(dd_hbm, 0, idx, 0, isem)
        first.start()
        first.wait()
        issue_rows(0)

        @pl.when(nb > 1)
        def _():
            _idx_copy(dd_hbm, 1, idx, 1, isem).start()

    @pl.when(i + 1 < nb)
    def _():
        _idx_copy(dd_hbm, i + 1, idx, 1 - slot, isem).wait()
        issue_rows(1 - slot)

    @pl.when(i + 2 < nb)
    def _():
        _idx_copy(dd_hbm, i + 2, idx, slot, isem).start()

    for r in range(TOP_K * R):
        row_copy(0, slot, r).wait()

    moe = w0_ref[...] * buf[slot, 0:R, :] + w1_ref[...] * buf[slot, R:2 * R, :]
    out_ref[...] = _rms(x2_ref[...] + moe, gf_ref[...])


def _combine(dd, y_disp, x2, w0, w1, g_final):
    T, D = x2.shape
    R = COMBINE_TILE
    row = lambda i: (i, 0)
    return pl.pallas_call(
        _combine_kernel,
        grid=(T // R,),
        in_specs=[
            pl.BlockSpec(memory_space=pl.ANY),
            pl.BlockSpec(memory_space=pl.ANY),
            pl.BlockSpec((R, D), row),
            pl.BlockSpec((R, 1), row),
            pl.BlockSpec((R, 1), row),
            pl.BlockSpec((1, D), lambda i: (0, 0)),
        ],
        out_specs=pl.BlockSpec((R, D), row),
        scratch_shapes=[pltpu.VMEM((2, TOP_K * R, D), F32), pltpu.SMEM((2, TOP_K * R), jnp.int32),
                        pltpu.SemaphoreType.DMA((2,)), pltpu.SemaphoreType.DMA((2,))],
        out_shape=jax.ShapeDtypeStruct((T, D), F32),
        compiler_params=pltpu.CompilerParams(
            dimension_semantics=("arbitrary",), vmem_limit_bytes=VMEM_LIMIT),
        name="combine",
    )(dd, y_disp, x2, w0, w1, g_final)


def _route_tables(route, counts_row, T):
    R = EXPERT_BLOCK
    counts = counts_row[0, 0:N_EXPERTS].astype(jnp.int32)
    padded = ((counts + R - 1) // R) * R
    pend = jnp.cumsum(padded)
    pstart = pend - padded
    nb = (T * TOP_K) // R + N_EXPERTS
    first_slot = jnp.arange(nb, dtype=jnp.int32) * R
    blk_e = jnp.sum((pend[None, :] <= first_slot[:, None]).astype(jnp.int32), axis=1)
    blk_e = jnp.minimum(blk_e, N_EXPERTS - 1)
    n_used = (pend[N_EXPERTS - 1:N_EXPERTS] // R).astype(jnp.int32)
    e = route[:, 2:4].astype(jnp.int32)
    rank = route[:, 4:6].astype(jnp.int32)
    sel = e[:, :, None] == jnp.arange(N_EXPERTS, dtype=jnp.int32)[None, None, :]
    dest = jnp.sum(jnp.where(sel, pstart[None, None, :], 0), axis=2) + rank
    d = dest.reshape(T // COMBINE_TILE, COMBINE_TILE, TOP_K)
    dd = jnp.transpose(d, (0, 2, 1)).reshape(T // COMBINE_TILE, TOP_K * COMBINE_TILE)
    total = pend[N_EXPERTS - 1:N_EXPERTS]
    pad_start = jnp.concatenate([pstart + counts, total])
    pad_n = jnp.concatenate([padded - counts, nb * R - total])
    return blk_e, n_used, dd, nb * R, pad_start, pad_n


def kernel(x, mem, g_mix, w_in, conv_w, conv_b, gate_b, g_head, ln_v_g, ln_v_b, w_s, b_s, w_out,
           g_xattn, g_mem, w_q_x, w_kv_x, w_o_x, g_moe, w_rg, b_rg, w_re, b_re,
           w_gate, w_up, w_down, g_final):
    B, S, D = x.shape
    T = B * S
    NC = S // CHUNK
    d_ml = N_HEADS * HEAD_DIM
    n_gates = 4 * N_HEADS
    assert w_in.shape[0] == 1, "one layer"
    assert S % PROJ_TILE == 0 and S % MIX_TILE == 0 and T % COMBINE_TILE == 0
    assert (T * TOP_K) % EXPERT_BLOCK == 0

    w = w_in[0]
    c_g = 4 * d_ml
    w_main = jnp.concatenate([w[:, 0:c_g], w[:, c_g + n_gates:]], axis=1).astype(BF16)
    wg = w[:, c_g:c_g + n_gates].reshape(D, 4, N_HEADS)
    wg = jnp.pad(jnp.transpose(wg, (2, 1, 0)), ((0, 0), (0, 4), (0, 0)))
    w_gt = wg.reshape(N_HEADS * 8, D).astype(BF16)
    gb = jnp.pad(jnp.transpose(gate_b[0].reshape(4, N_HEADS), (1, 0)), ((0, 0), (0, 4)))
    gb = gb.reshape(N_HEADS * 8, 1).astype(F32)

    qk, v, o, gu, gv, gt = _proj_in(x, g_mix[0:1], w_main, w_gt, gb)

    h_ml = _mlstm(qk.reshape(B, S, 2 * d_ml), v.reshape(B, NC, CHUNK, d_ml), gt,
                  conv_w[0].reshape(CONV_WIDTH, 2 * d_ml), conv_b[0:1], g_head[0:1])
    h_ml = h_ml.reshape(T, d_ml)

    dh = D // N_XHEADS
    w_kv = w_kv_x[0]
    w_kt = jnp.transpose(w_kv[:, 0:D].reshape(D, N_XHEADS, dh), (1, 2, 0)).astype(BF16)
    w_v = jnp.transpose(w_kv[:, D:2 * D].reshape(D, N_XHEADS, dh), (1, 0, 2)).astype(BF16)
    kt, vm = _kv_proj(mem, g_mem[0:1], w_kt, w_v)

    w_r = jnp.concatenate([w_rg[0], w_re[0]], axis=1)
    w_r = jnp.pad(w_r, ((0, 0), (0, LANES - w_r.shape[1]))).astype(BF16)
    b_r = jnp.concatenate([b_rg[0], b_re[0]])
    b_r = jnp.pad(b_r, (0, LANES - b_r.shape[0])).reshape(1, LANES).astype(F32)
    bs_b = jnp.broadcast_to(b_s[0][:, :, None], (N_GROUPS, CHUNK, HEAD_DIM)).astype(F32)

    x2, hn, route, counts = _mix_attn(
        x.reshape(T, D), h_ml, o, gu, gv, ln_v_g[0:1], ln_v_b[0:1], w_s[0].astype(BF16), bs_b,
        w_out[0].astype(BF16), g_xattn[0:1], w_q_x[0].astype(BF16), kt, vm, w_o_x[0].astype(BF16),
        g_moe[0:1], w_r, b_r, S)

    blk_e, n_used, dd, n_slots, pad_start, pad_n = _route_tables(route, counts, T)
    x_disp = _dispatch(pad_start, pad_n, dd, hn, n_slots)

    w_gu = jnp.concatenate([w_gate[0], w_up[0]], axis=2).astype(BF16)
    y_disp = _experts(blk_e, n_used, x_disp, w_gu, w_down[0].astype(BF16))

    out = _combine(dd, y_disp, x2, route[:, 0:1], route[:, 1:2], g_final.reshape(1, D))
    return out.reshape(B, S, D)
```

```python
import jax
import jax.numpy as jnp
from jax import lax
from jax.experimental import pallas as pl
from jax.experimental.pallas import tpu as pltpu

F32 = jnp.float32
BF16 = jnp.bfloat16

RMS_EPS = 1e-6
LN_EPS = 1e-5
NEG_INIT = -1e30

N_HEADS = 4
HEAD_DIM = 128
CHUNK = 128
CONV_WIDTH = 5
HALO = 8
N_GROUPS = 4
N_XHEADS = 4
N_EXPERT_GROUPS = 4
EXPERTS_PER_GROUP = 8
N_EXPERTS = N_EXPERT_GROUPS * EXPERTS_PER_GROUP
TOP_K = 2
LANES = 128

PROJ_TILE = 512
MIX_TILE = 512
MIX_SUB = 512
EXPERT_BLOCK = 256
COMBINE_TILE = 256
CONV_ROWS = 64
HEADS_PER_STEP = 2
MLSTM_UNROLL = 4
VMEM_LIMIT = 56 * 1024 * 1024


def _dot(a, b):
    return jnp.dot(a, b, preferred_element_type=F32)


def _dot_nt(a, b):
    return lax.dot_general(a, b, (((1,), (1,)), ((), ())), preferred_element_type=F32)


def _rms(x, g):
    return x * lax.rsqrt(jnp.mean(x * x, axis=-1, keepdims=True) + RMS_EPS) * g


def _sigmoid(x):
    return 1.0 / (1.0 + jnp.exp(-x))


def _silu(x):
    return x * _sigmoid(x)


def _gelu_tanh(x):
    c = 0.7978845608028654
    return 0.5 * x * (1.0 + jnp.tanh(c * (x + 0.044715 * (x * x * x))))


def _log_sigmoid(x):
    return jnp.minimum(x, 0.0) - jnp.log(1.0 + jnp.exp(-jnp.abs(x)))


def _proj_in_kernel(x_ref, g_ref, w_ref, wgt_ref, gb_ref,
                    qk_ref, v_ref, o_ref, gu_ref, gv_ref, gt_ref):
    xb = _rms(x_ref[...], g_ref[...]).astype(BF16)
    qk_ref[...] = _dot(xb, w_ref[:, 0:1024])
    v_ref[...] = _dot(xb, w_ref[:, 1024:1536]).astype(BF16)
    o_ref[...] = _dot(xb, w_ref[:, 1536:2048])
    gu_ref[...] = _dot(xb, w_ref[:, 2048:2560])
    gv_ref[...] = _dot(xb, w_ref[:, 2560:3072])
    gt = _dot_nt(wgt_ref[...], xb) + gb_ref[...]
    for h in range(N_HEADS):
        for j in range(gt_ref.shape[1]):
            gt_ref[h, j] = gt[8 * h:8 * h + 8, CHUNK * j:CHUNK * (j + 1)]


def _proj_in(x, g_mix, w_main, w_gt, gate_b):
    B, S, D = x.shape
    tm = PROJ_TILE
    nj = tm // CHUNK
    NC = S // CHUNK
    T = B * S
    grid = (B, S // tm)
    row = lambda b, i: (b * (S // tm) + i, 0)
    const = lambda b, i: (0, 0)
    outs = pl.pallas_call(
        _proj_in_kernel,
        grid=grid,
        in_specs=[
            pl.BlockSpec((None, tm, D), lambda b, i: (b, i, 0)),
            pl.BlockSpec((1, D), const),
            pl.BlockSpec(w_main.shape, const),
            pl.BlockSpec(w_gt.shape, const),
            pl.BlockSpec(gate_b.shape, const),
        ],
        out_specs=[
            pl.BlockSpec((tm, 1024), row),
            pl.BlockSpec((tm, 512), row),
            pl.BlockSpec((tm, 512), row),
            pl.BlockSpec((tm, 512), row),
            pl.BlockSpec((tm, 512), row),
            pl.BlockSpec((None, N_HEADS, nj, 8, CHUNK), lambda b, i: (b, 0, i, 0, 0)),
        ],
        out_shape=[
            jax.ShapeDtypeStruct((T, 1024), F32),
            jax.ShapeDtypeStruct((T, 512), BF16),
            jax.ShapeDtypeStruct((T, 512), F32),
            jax.ShapeDtypeStruct((T, 512), F32),
            jax.ShapeDtypeStruct((T, 512), F32),
            jax.ShapeDtypeStruct((B, N_HEADS, NC, 8, CHUNK), F32),
        ],
        compiler_params=pltpu.CompilerParams(
            dimension_semantics=("parallel", "parallel"), vmem_limit_bytes=VMEM_LIMIT),
        name="proj_in",
    )(x, g_mix, w_main, w_gt, gate_b)
    return outs


def _conv_silu(src_ref, w, bias, r0, n_rows):
    S = src_ref.shape[0]
    acc = None
    for j in range(CONV_WIDTH):
        d = j - CONV_WIDTH // 2
        if r0 + d < 0 or r0 + d + n_rows > S:
            blk = src_ref[r0:r0 + n_rows, :]
            ridx = lax.broadcasted_iota(jnp.int32, blk.shape, 0)
            sh = pltpu.roll(blk, (-d) % n_rows, axis=0)
            tap = jnp.where((ridx + d >= 0) & (ridx + d < n_rows), sh, 0.0)
        else:
            tap = src_ref[r0 + d:r0 + d + n_rows, :]
        term = w[j:j + 1, :] * tap
        acc = term if acc is None else acc + term
    return _silu(acc + bias)


G_B, G_R, G_CM, G_WK, G_F, G_ML = range(6)


def _gate_rows(g, t_idx, s_idx):
    n = g.shape[0]
    ls = _log_sigmoid(g)
    hi = ls.astype(BF16).astype(F32)
    r1 = ls - hi
    mid = r1.astype(BF16).astype(F32)
    lo = (r1 - mid).astype(BF16).astype(F32)
    parts = jnp.concatenate([hi, mid, lo], axis=0).astype(BF16)
    one = lambda m: jnp.where(m, 1.0, 0.0).astype(BF16)
    rhs = jnp.concatenate([one(t_idx <= s_idx), one(t_idx >= s_idx), jnp.ones((CHUNK, CHUNK), BF16)], axis=1)
    cs = _dot(parts, rhs)
    cum = cs[0:n] + cs[n:2 * n] + cs[2 * n:3 * n]
    up1 = lambda a: pltpu.roll(a, n - 1, axis=0)
    fwd_row = lax.rem(lax.broadcasted_iota(jnp.int32, (n, CHUNK), 0), 8) == 0
    lane = lax.broadcasted_iota(jnp.int32, (n, CHUNK), 1)
    b = jnp.where(fwd_row, up1(cum[:, 0:CHUNK]), up1(cum[:, CHUNK:2 * CHUNK]))
    r = g - b
    f = up1(cum[:, 2 * CHUNK:3 * CHUNK])
    a = f + r
    ml = jnp.broadcast_to(jnp.max(a, axis=1, keepdims=True), a.shape)
    wk = jnp.exp(a - ml)
    pm = r
    sm = r
    s = 1
    while s < CHUNK:
        pm = jnp.maximum(pm, jnp.where(lane >= s, pltpu.roll(pm, s, axis=1), -jnp.inf))
        sm = jnp.maximum(sm, jnp.where(lane < CHUNK - s, pltpu.roll(sm, CHUNK - s, axis=1), -jnp.inf))
        s *= 2
    cm = jnp.where(fwd_row, pm, sm)
    return b, r, cm, wk, f, ml


def _mlstm_kernel(q_ref, k_ref, v_ref, g_ref, cwq_ref, cwk_ref, cbq_ref, cbk_ref, gh_ref,
                  out_ref, qc_ref, kt_ref, gate_ref, caug_ref, m_ref):
    NC = v_ref.shape[0]
    hps = HEADS_PER_STEP
    k_scale = HEAD_DIM ** -0.5

    cwq, cwk = cwq_ref[...], cwk_ref[...]
    cbq, cbk = cbq_ref[...], cbk_ref[...]
    n_sub = CHUNK // CONV_ROWS
    for c in range(NC):
        qb = jnp.concatenate([_conv_silu(q_ref, cwq, cbq, c * CHUNK + i * CONV_ROWS, CONV_ROWS)
                              for i in range(n_sub)], axis=0)
        kb = jnp.concatenate([_conv_silu(k_ref, cwk, cbk, c * CHUNK + i * CONV_ROWS, CONV_ROWS)
                              for i in range(n_sub)], axis=0) * k_scale
        for hd in range(hps):
            cols = slice(hd * HEAD_DIM, (hd + 1) * HEAD_DIM)
            qc_ref[hd, c] = qb[:, cols].astype(BF16)
            kt_ref[hd, c] = kb[:, cols].T

    t_idx = lax.broadcasted_iota(jnp.int32, (CHUNK, CHUNK), 0)
    s_idx = lax.broadcasted_iota(jnp.int32, (CHUNK, CHUNK), 1)
    ones = jnp.ones((CHUNK, HEAD_DIM), BF16)
    gh = gh_ref[...]

    for hd in range(hps):
        rows = _gate_rows(g_ref[hd].reshape(NC * 8, CHUNK), t_idx, s_idx)
        for kind, val in enumerate(rows):
            gate_ref[kind, hd] = val.reshape(NC, 8, CHUNK)

    def col_form(row):
        return jnp.broadcast_to(row, (CHUNK, CHUNK)).T

    def chunk_step(hd, c, direction):
        st = 2 * hd + direction
        row = slice(2 * direction, 2 * direction + 1)
        b_r = gate_ref[G_B, hd, c][row]
        r_r = gate_ref[G_R, hd, c][row]
        cm_r = gate_ref[G_CM, hd, c][row]
        wk_r = gate_ref[G_WK, hd, c][row]
        f_tot = gate_ref[G_F, hd, c][row]
        m_loc = gate_ref[G_ML, hd, c][row]
        causal = (s_idx <= t_idx) if direction == 0 else (s_idx >= t_idx)

        m_prev = m_ref[st][0:1]
        caug = caug_ref[st]
        q = qc_ref[hd, c]
        kt = kt_ref[hd, c]
        vaug = jnp.concatenate([v_ref[c, :, hd * HEAD_DIM:(hd + 1) * HEAD_DIM], ones], axis=1)

        b_c = col_form(b_r)
        mx_c = jnp.maximum(col_form(cm_r), m_prev)
        w_mat = jnp.where(causal, jnp.exp(r_r - mx_c), 0.0)
        s_inter = jnp.exp(m_prev - mx_c)
        scores = _dot(q, kt.astype(BF16))
        wqk = (w_mat * scores).astype(BF16)
        intra = _dot(wqk, vaug)
        carried = _dot(q, caug.astype(BF16))
        num = intra[:, 0:HEAD_DIM] + s_inter * carried[:, 0:HEAD_DIM]
        den = intra[:, HEAD_DIM:2 * HEAD_DIM] + s_inter * carried[:, HEAD_DIM:2 * HEAD_DIM]
        h = num / jnp.maximum(jnp.abs(den), jnp.exp(-(b_c + mx_c)))

        kw = (kt * wk_r).astype(BF16)
        c_loc = _dot(kw, vaug)
        m_new = jnp.maximum(f_tot + m_prev, m_loc)
        s_old = jnp.exp(f_tot + m_prev - m_new)
        s_new = jnp.exp(m_loc - m_new)
        wide = lambda a: jnp.concatenate([a, a], axis=1)
        caug_ref[st] = wide(s_old) * caug + wide(s_new) * c_loc
        m_ref[st] = jnp.broadcast_to(m_new, m_ref.shape[1:])
        return h

    caug_ref[...] = jnp.zeros_like(caug_ref)
    m_ref[...] = jnp.full(m_ref.shape, NEG_INIT, F32)

    def finish(hd, hs):
        cols = slice(hd * HEAD_DIM, (hd + 1) * HEAD_DIM)
        return hs * lax.rsqrt(jnp.mean(hs * hs, axis=-1, keepdims=True) + RMS_EPS) * gh[:, cols]

    def make_body(final):
        def body(i, carry):
            cf = i
            cb = NC - 1 - i
            for hd in range(hps):
                cols = slice(hd * HEAD_DIM, (hd + 1) * HEAD_DIM)
                h_f = chunk_step(hd, cf, 0)
                h_b = chunk_step(hd, cb, 1)
                if final:
                    out_ref[cf, :, cols] = finish(hd, out_ref[cf, :, cols] + h_f)
                    out_ref[cb, :, cols] = finish(hd, out_ref[cb, :, cols] + h_b)
                else:
                    out_ref[cf, :, cols] = h_f
                    out_ref[cb, :, cols] = h_b
            return carry
        return body

    lax.fori_loop(0, NC // 2, make_body(False), 0, unroll=MLSTM_UNROLL)
    lax.fori_loop(NC // 2, NC, make_body(True), 0, unroll=MLSTM_UNROLL)


def _mlstm(qk, v, gt, conv_w, conv_b, g_head):
    B, S = qk.shape[0], qk.shape[1]
    NC = S // CHUNK
    assert NC % 2 == 0
    hps = HEADS_PER_STEP
    w = hps * HEAD_DIM
    k_off = N_HEADS // hps
    seq = lambda off: pl.BlockSpec((None, S, w), lambda b, p: (b, 0, p + off))
    blk = lambda off: pl.BlockSpec((None, NC, CHUNK, w), lambda b, p: (b, 0, 0, p + off))
    par = lambda r, off: pl.BlockSpec((r, w), lambda b, p: (0, p + off))
    return pl.pallas_call(
        _mlstm_kernel,
        grid=(B, N_HEADS // hps),
        in_specs=[
            seq(0), seq(k_off), blk(0),
            pl.BlockSpec((None, hps, NC, 8, CHUNK), lambda b, p: (b, p, 0, 0, 0)),
            par(CONV_WIDTH, 0), par(CONV_WIDTH, k_off), par(1, 0), par(1, k_off), par(1, 0),
        ],
        out_specs=blk(0),
        out_shape=jax.ShapeDtypeStruct(v.shape, F32),
        scratch_shapes=[
            pltpu.VMEM((hps, NC, CHUNK, HEAD_DIM), BF16),
            pltpu.VMEM((hps, NC, HEAD_DIM, CHUNK), F32),
            pltpu.VMEM((6, hps, NC, 8, CHUNK), F32),
            pltpu.VMEM((2 * hps, HEAD_DIM, 2 * HEAD_DIM), F32),
            pltpu.VMEM((2 * hps, 8, LANES), F32),
        ],
        compiler_params=pltpu.CompilerParams(
            dimension_semantics=("parallel", "parallel"), vmem_limit_bytes=VMEM_LIMIT),
        name="mlstm",
    )(qk, qk, v, gt, conv_w, conv_w, conv_b, conv_b, g_head)


def _kv_kernel(mem_ref, g_ref, wkt_ref, wv_ref, kt_ref, v_ref):
    mn = _rms(mem_ref[...], g_ref[...]).astype(BF16)
    for h in range(N_XHEADS):
        kt_ref[h] = _dot_nt(wkt_ref[h], mn).astype(BF16)
        v_ref[h] = _dot(mn, wv_ref[h]).astype(BF16)


def _kv_proj(mem, g_mem, w_kt, w_v):
    B, M, D = mem.shape
    dh = D // N_XHEADS
    return pl.pallas_call(
        _kv_kernel,
        grid=(B,),
        in_specs=[
            pl.BlockSpec((None, M, D), lambda b: (b, 0, 0)),
            pl.BlockSpec((1, D), lambda b: (0, 0)),
            pl.BlockSpec(w_kt.shape, lambda b: (0, 0, 0)),
            pl.BlockSpec(w_v.shape, lambda b: (0, 0, 0)),
        ],
        out_specs=[
            pl.BlockSpec((None, N_XHEADS, dh, M), lambda b: (b, 0, 0, 0)),
            pl.BlockSpec((None, N_XHEADS, M, dh), lambda b: (b, 0, 0, 0)),
        ],
        out_shape=[
            jax.ShapeDtypeStruct((B, N_XHEADS, dh, M), BF16),
            jax.ShapeDtypeStruct((B, N_XHEADS, M, dh), BF16),
        ],
        compiler_params=pltpu.CompilerParams(
            dimension_semantics=("parallel",), vmem_limit_bytes=VMEM_LIMIT),
        name="kv_proj",
    )(mem, g_mem, w_kt, w_v)


def _mix_rows(r0, n, x_ref, hm_ref, o_ref, gu_ref, gv_ref, lng_ref, lnb_ref, ws_ref, bs_ref, wout_ref,
              gx_ref, wq_ref, kt_ref, vm_ref, wo_ref, gmoe_ref, wr_ref, br_ref, x2_ref, hn_ref):
    rs = slice(r0, r0 + n)
    gd = HEAD_DIM
    y_mlstm = _sigmoid(o_ref[rs, :]) * hm_ref[rs, :]
    gu = _gelu_tanh(gu_ref[rs, :])
    gv = _gelu_tanh(gv_ref[rs, :])
    mu = jnp.mean(gv, axis=-1, keepdims=True)
    gc = gv - mu
    gvn = gc * lax.rsqrt(jnp.mean(gc * gc, axis=-1, keepdims=True) + LN_EPS) * lng_ref[...] + lnb_ref[...]
    gvb = gvn.astype(BF16)
    rows = []
    for j in range(n // CHUNK):
        cols = []
        for g in range(N_GROUPS):
            sp = _dot(ws_ref[g], gvb[j * CHUNK:(j + 1) * CHUNK, g * gd:(g + 1) * gd]) + bs_ref[g]
            cols.append(sp)
        rows.append(jnp.concatenate(cols, axis=1))
    y_gmlp = gu * jnp.concatenate(rows, axis=0)
    n_ml = hm_ref.shape[1]
    mix = _dot(y_mlstm.astype(BF16), wout_ref[0:n_ml, :]) + _dot(y_gmlp.astype(BF16), wout_ref[n_ml:, :])
    x1 = x_ref[rs, :] + mix

    hq = _rms(x1, gx_ref[...]).astype(BF16)
    q = _dot(hq, wq_ref[...])
    dh = kt_ref.shape[1]
    scale = dh ** -0.5
    heads = []
    for h in range(N_XHEADS):
        s = _dot(q[:, h * dh:(h + 1) * dh].astype(BF16), kt_ref[h]) * scale
        e = jnp.exp(s - jnp.max(s, axis=-1, keepdims=True))
        p = e / jnp.sum(e, axis=-1, keepdims=True)
        heads.append(_dot(p.astype(BF16), vm_ref[h]))
    att = jnp.concatenate(heads, axis=1).astype(BF16)
    x2 = x1 + _dot(att, wo_ref[...])
    x2_ref[rs, :] = x2

    hn = _rms(x2, gmoe_ref[...])
    hn_ref[rs, :] = hn
    return _dot_nt(wr_ref[...], hn.astype(BF16)) + br_ref[...]


def _mix_attn_kernel(x_ref, hm_ref, o_ref, gu_ref, gv_ref, lng_ref, lnb_ref, ws_ref, bs_ref, wout_ref,
                     gx_ref, wq_ref, kt_ref, vm_ref, wo_ref, gmoe_ref, wr_ref, br_ref, utri_ref,
                     x2_ref, hn_ref, rt_ref, cnt_ref, run_ref):
    tm = x_ref.shape[0]
    parts = [_mix_rows(r0, MIX_SUB, x_ref, hm_ref, o_ref, gu_ref, gv_ref, lng_ref, lnb_ref, ws_ref, bs_ref,
                       wout_ref, gx_ref, wq_ref, kt_ref, vm_ref, wo_ref, gmoe_ref, wr_ref, br_ref,
                       x2_ref, hn_ref)
             for r0 in range(0, tm, MIX_SUB)]
    n_log = N_EXPERT_GROUPS + N_EXPERTS + 4
    lg = jnp.concatenate(parts, axis=1)[0:n_log]

    rowi = lax.broadcasted_iota(jnp.int32, lg.shape, 0).astype(F32)
    none = float(LANES)
    gmask = rowi < N_EXPERT_GROUPS
    gmax = jnp.max(jnp.where(gmask, lg, -jnp.inf), axis=0, keepdims=True)
    ge = jnp.where(gmask, jnp.exp(lg - gmax), 0.0)
    p = ge / jnp.sum(ge, axis=0, keepdims=True)
    p_top = jnp.max(p, axis=0, keepdims=True)
    g_idx = jnp.min(jnp.where(jnp.where(gmask, p, -1.0) == p_top, rowi, none), axis=0, keepdims=True)
    lo = N_EXPERT_GROUPS + EXPERTS_PER_GROUP * g_idx
    el = jnp.where(rowi >= lo, jnp.where(rowi < lo + EXPERTS_PER_GROUP, lg, -jnp.inf), -jnp.inf)
    v1 = jnp.max(el, axis=0, keepdims=True)
    i1 = jnp.min(jnp.where(el == v1, rowi, none), axis=0, keepdims=True)
    el2 = jnp.where(rowi == i1, -jnp.inf, el)
    v2 = jnp.max(el2, axis=0, keepdims=True)
    i2 = jnp.min(jnp.where(el2 == v2, rowi, none), axis=0, keepdims=True)
    ex2 = jnp.exp(v2 - v1)
    den = 1.0 + ex2
    w1 = p_top * (1.0 / den)
    w2 = p_top * (ex2 / den)
    e1 = i1 - N_EXPERT_GROUPS
    e2 = i2 - N_EXPERT_GROUPS

    @pl.when(pl.program_id(0) == 0)
    def _():
        run_ref[...] = jnp.zeros_like(run_ref)

    erow = lax.broadcasted_iota(jnp.int32, (N_EXPERTS, tm), 0).astype(F32)
    is1 = erow == e1
    is2 = erow == e2
    oh = jnp.where(is1, 1.0, jnp.where(is2, 1.0, 0.0))
    run = run_ref[...]
    before = _dot(oh.astype(BF16), utri_ref[...]) + jnp.concatenate([run] * (tm // LANES), axis=1)
    rank1 = jnp.sum(jnp.where(is1, before, 0.0), axis=0, keepdims=True)
    rank2 = jnp.sum(jnp.where(is2, before, 0.0), axis=0, keepdims=True)
    total = run + jnp.sum(oh, axis=1, keepdims=True)
    run_ref[...] = total
    cnt_ref[...] = total
    zero = jnp.zeros_like(w1)
    rt_ref[...] = jnp.concatenate([w1, w2, e1, e2, rank1, rank2, zero, zero], axis=0)


def _mix_attn(x, hm, o, gu, gv, ln_g, ln_b, w_s, b_s, w_out, g_x, w_q, kt, vm, w_o, g_moe, w_r, b_r, S):
    T, D = x.shape
    tm = MIX_TILE
    per_b = S // tm
    row = lambda i: (i, 0)
    c2 = lambda i: (0, 0)
    c3 = lambda i: (0, 0, 0)
    full = lambda a: pl.BlockSpec(a.shape, c2 if a.ndim == 2 else c3)
    tile = lambda a: pl.BlockSpec((tm, a.shape[1]), row)
    ti = lax.broadcasted_iota(jnp.int32, (tm, tm), 0)
    si = lax.broadcasted_iota(jnp.int32, (tm, tm), 1)
    utri = jnp.where(ti < si, 1.0, 0.0).astype(BF16)
    return pl.pallas_call(
        _mix_attn_kernel,
        grid=(T // tm,),
        in_specs=[
            tile(x), tile(hm), tile(o), tile(gu), tile(gv),
            full(ln_g), full(ln_b), full(w_s), full(b_s), full(w_out),
            full(g_x), full(w_q),
            pl.BlockSpec((None,) + kt.shape[1:], lambda i: (i // per_b, 0, 0, 0)),
            pl.BlockSpec((None,) + vm.shape[1:], lambda i: (i // per_b, 0, 0, 0)),
            full(w_o), full(g_moe), full(w_r), full(b_r), full(utri),
        ],
        out_specs=[pl.BlockSpec((tm, D), row), pl.BlockSpec((tm, D), row), pl.BlockSpec((8, tm), lambda i: (0, i)),
                   pl.BlockSpec((N_EXPERTS, LANES), c2)],
        out_shape=[
            jax.ShapeDtypeStruct((T, D), F32),
            jax.ShapeDtypeStruct((T, D), F32),
            jax.ShapeDtypeStruct((8, T), F32),
            jax.ShapeDtypeStruct((N_EXPERTS, LANES), F32),
        ],
        scratch_shapes=[pltpu.VMEM((N_EXPERTS, LANES), F32)],
        compiler_params=pltpu.CompilerParams(
            dimension_semantics=("arbitrary",), vmem_limit_bytes=VMEM_LIMIT),
        name="mix_attn",
    )(x, hm, o, gu, gv, ln_g, ln_b, w_s, b_s, w_out, g_x, w_q, kt, vm, w_o, g_moe, w_r, b_r, utri)


def _idx_copy(idx_hbm, block, idx_smem, slot, isem):
    return pltpu.make_async_copy(idx_hbm.at[block], idx_smem.at[slot], isem.at[slot])


def _dispatch_kernel(pad_start_ref, pad_n_ref, dd_hbm, h_ref, xd_hbm, idx, zrow, sem, isem, zsem):
    i = pl.program_id(0)
    nb = pl.num_programs(0)
    R = h_ref.shape[0]
    slot = lax.rem(i, 2)

    def row_copy(r, dst):
        return pltpu.make_async_copy(h_ref.at[pl.ds(r, 1), :], xd_hbm.at[pl.ds(dst, 1), :], sem)

    def zero_copy(dst):
        return pltpu.make_async_copy(zrow.at[pl.ds(0, 1), :], xd_hbm.at[pl.ds(dst, 1), :], zsem)

    @pl.when(i == 0)
    def _():
        first = _idx_copy(dd_hbm, 0, idx, 0, isem)
        first.start()
        zrow[...] = jnp.zeros_like(zrow)
        for e in range(pad_start_ref.shape[0]):
            base = pad_start_ref[e]
            count = pad_n_ref[e]

            def start_body(j, carry, base=base):
                zero_copy(base + j).start()
                return carry

            def wait_body(j, carry):
                zero_copy(0).wait()
                return carry

            lax.fori_loop(0, count, start_body, 0)
            lax.fori_loop(0, count, wait_body, 0)
        first.wait()

    @pl.when(i + 1 < nb)
    def _():
        _idx_copy(dd_hbm, i + 1, idx, 1 - slot, isem).start()

    for s in range(2):
        @pl.when(slot == s)
        def _(s=s):
            for r in range(R):
                for k in range(TOP_K):
                    row_copy(r, idx[s, k * R + r]).start()
    for r in range(TOP_K * R):
        row_copy(0, 0).wait()

    @pl.when(i + 1 < nb)
    def _():
        _idx_copy(dd_hbm, i + 1, idx, 1 - slot, isem).wait()


def _dispatch(pad_start, pad_n, dd, hn, n_slots):
    T, D = hn.shape
    nt, two_r = dd.shape
    R = two_r // TOP_K
    any_spec = pl.BlockSpec(memory_space=pl.ANY)
    return pl.pallas_call(
        _dispatch_kernel,
        grid_spec=pltpu.PrefetchScalarGridSpec(
            num_scalar_prefetch=2,
            grid=(nt,),
            in_specs=[any_spec, pl.BlockSpec((R, D), lambda i, ps, pn: (i, 0))],
            out_specs=any_spec,
            scratch_shapes=[pltpu.SMEM((2, two_r), jnp.int32), pltpu.VMEM((8, D), hn.dtype),
                            pltpu.SemaphoreType.DMA(()), pltpu.SemaphoreType.DMA((2,)),
                            pltpu.SemaphoreType.DMA(())],
        ),
        out_shape=jax.ShapeDtypeStruct((n_slots, D), hn.dtype),
        compiler_params=pltpu.CompilerParams(
            dimension_semantics=("arbitrary",), vmem_limit_bytes=VMEM_LIMIT),
        name="dispatch",
    )(pad_start, pad_n, dd, hn)


def _expert_kernel(blk_e_ref, n_used_ref, x_ref, wg_ref, wu_ref, wd_ref, y_ref):
    used = pl.program_id(0) < n_used_ref[0]

    @pl.when(used)
    def _():
        xb = x_ref[...].astype(BF16)
        act = (_silu(_dot(xb, wg_ref[...])) * _dot(xb, wu_ref[...])).astype(BF16)
        y_ref[...] = _dot(act, wd_ref[...])

    @pl.when(jnp.logical_not(used))
    def _():
        y_ref[...] = jnp.zeros_like(y_ref)


def _experts(blk_e, n_used, x_disp, w_g, w_u, w_d):
    P, D = x_disp.shape
    R = EXPERT_BLOCK
    nb = P // R
    de = w_d.shape[1]
    last = lambda i, nu: jnp.minimum(i, nu[0] - 1)
    return pl.pallas_call(
        _expert_kernel,
        grid_spec=pltpu.PrefetchScalarGridSpec(
            num_scalar_prefetch=2,
            grid=(nb,),
            in_specs=[
                pl.BlockSpec((R, D), lambda i, be, nu: (last(i, nu), 0)),
                pl.BlockSpec((None, D, de), lambda i, be, nu: (be[last(i, nu)], 0, 0)),
                pl.BlockSpec((None, D, de), lambda i, be, nu: (be[last(i, nu)], 0, 0)),
                pl.BlockSpec((None, de, D), lambda i, be, nu: (be[last(i, nu)], 0, 0)),
            ],
            out_specs=pl.BlockSpec((R, D), lambda i, be, nu: (i, 0)),
        ),
        out_shape=jax.ShapeDtypeStruct((P, D), F32),
        compiler_params=pltpu.CompilerParams(
            dimension_semantics=("arbitrary",), vmem_limit_bytes=VMEM_LIMIT),
        name="experts",
    )(blk_e, n_used, x_disp, w_g, w_u, w_d)


def _combine_kernel(dd_hbm, y_hbm, x2_ref, rt_ref, gf_ref, out_ref, buf, idx, sem, isem):
    i = pl.program_id(0)
    nb = pl.num_programs(0)
    R = x2_ref.shape[0]
    slot = lax.rem(i, 2)

    def row_copy(src, s, r):
        return pltpu.make_async_copy(y_hbm.at[pl.ds(src, 1), :], buf.at[s, pl.ds(r, 1), :], sem.at[s])

    def issue_rows(s):
        for r in range(TOP_K * R):
            row_copy(idx[s, r], s, r).start()

    @pl.when(i == 0)
    def _():
        first = _idx_copy(dd_hbm, 0, idx, 0, isem)
        first.start()
        first.wait()
        issue_rows(0)

        @pl.when(nb > 1)
        def _():
            _idx_copy(dd_hbm, 1, idx, 1, isem).start()

    for s in range(2):
        @pl.when((i + 1 < nb) & (slot == 1 - s))
        def _(s=s):
            _idx_copy(dd_hbm, i + 1, idx, s, isem).wait()
            issue_rows(s)

    @pl.when(i + 2 < nb)
    def _():
        _idx_copy(dd_hbm, i + 2, idx, slot, isem).start()

    for r in range(TOP_K * R):
        row_copy(0, slot, r).wait()

    rt = rt_ref[...]
    D = x2_ref.shape[1]
    for j in range(R // LANES):
        rows = slice(j * LANES, (j + 1) * LANES)
        wa = jnp.broadcast_to(rt[0:1, rows], (LANES, LANES)).T
        wb = jnp.broadcast_to(rt[1:2, rows], (LANES, LANES)).T
        wa = jnp.concatenate([wa] * (D // LANES), axis=1)
        wb = jnp.concatenate([wb] * (D // LANES), axis=1)
        ya = buf[slot, j * LANES:(j + 1) * LANES, :]
        yb = buf[slot, R + j * LANES:R + (j + 1) * LANES, :]
        out_ref[rows, :] = _rms(x2_ref[rows, :] + (wa * ya + wb * yb), gf_ref[...])


def _combine(dd, y_disp, x2, route, g_final):
    T, D = x2.shape
    R = COMBINE_TILE
    row = lambda i: (i, 0)
    return pl.pallas_call(
        _combine_kernel,
        grid=(T // R,),
        in_specs=[
            pl.BlockSpec(memory_space=pl.ANY),
            pl.BlockSpec(memory_space=pl.ANY),
            pl.BlockSpec((R, D), row),
            pl.BlockSpec((8, R), lambda i: (0, i)),
            pl.BlockSpec((1, D), lambda i: (0, 0)),
        ],
        out_specs=pl.BlockSpec((R, D), row),
        scratch_shapes=[pltpu.VMEM((2, TOP_K * R, D), F32), pltpu.SMEM((2, TOP_K * R), jnp.int32),
                        pltpu.SemaphoreType.DMA((2,)), pltpu.SemaphoreType.DMA((2,))],
        out_shape=jax.ShapeDtypeStruct((T, D), F32),
        compiler_params=pltpu.CompilerParams(
            dimension_semantics=("arbitrary",), vmem_limit_bytes=VMEM_LIMIT),
        name="combine",
    )(dd, y_disp, x2, route, g_final)


def _route_tables(route, counts_row, T):
    R = EXPERT_BLOCK
    counts = counts_row[:, 0].astype(jnp.int32)
    padded = ((counts + R - 1) // R) * R
    pend = jnp.cumsum(padded)
    pstart = pend - padded
    nb = (T * TOP_K) // R + N_EXPERTS
    first_slot = jnp.arange(nb, dtype=jnp.int32) * R
    blk_e = jnp.sum((pend[None, :] <= first_slot[:, None]).astype(jnp.int32), axis=1)
    blk_e = jnp.minimum(blk_e, N_EXPERTS - 1)
    n_used = (pend[N_EXPERTS - 1:N_EXPERTS] // R).astype(jnp.int32)
    e = route[2:4].astype(jnp.int32)
    rank = route[4:6].astype(jnp.int32)
    sel = e[None, :, :] == jnp.arange(N_EXPERTS, dtype=jnp.int32)[:, None, None]
    dest = jnp.sum(jnp.where(sel, pstart[:, None, None], 0), axis=0) + rank
    d = dest.reshape(TOP_K, T // COMBINE_TILE, COMBINE_TILE)
    dd = jnp.transpose(d, (1, 0, 2)).reshape(T // COMBINE_TILE, TOP_K * COMBINE_TILE)
    total = pend[N_EXPERTS - 1:N_EXPERTS]
    pad_start = jnp.concatenate([pstart + counts, total])
    pad_n = jnp.concatenate([padded - counts, nb * R - total])
    return blk_e, n_used, dd, nb * R, pad_start, pad_n


def kernel(x, mem, g_mix, w_in, conv_w, conv_b, gate_b, g_head, ln_v_g, ln_v_b, w_s, b_s, w_out,
           g_xattn, g_mem, w_q_x, w_kv_x, w_o_x, g_moe, w_rg, b_rg, w_re, b_re,
           w_gate, w_up, w_down, g_final):
    B, S, D = x.shape
    T = B * S
    NC = S // CHUNK
    d_ml = N_HEADS * HEAD_DIM
    n_gates = 4 * N_HEADS
    assert w_in.shape[0] == 1, "one layer"
    assert S % PROJ_TILE == 0 and S % MIX_TILE == 0 and T % COMBINE_TILE == 0
    assert (T * TOP_K) % EXPERT_BLOCK == 0

    w = w_in[0]
    c_g = 4 * d_ml
    w_main = jnp.concatenate([w[:, 0:c_g], w[:, c_g + n_gates:]], axis=1).astype(BF16)
    wg = w[:, c_g:c_g + n_gates].reshape(D, 4, N_HEADS)
    wg = jnp.pad(jnp.transpose(wg, (2, 1, 0)), ((0, 0), (0, 4), (0, 0)))
    w_gt = wg.reshape(N_HEADS * 8, D).astype(BF16)
    gb = jnp.pad(jnp.transpose(gate_b[0].reshape(4, N_HEADS), (1, 0)), ((0, 0), (0, 4)))
    gb = gb.reshape(N_HEADS * 8, 1).astype(F32)

    qk, v, o, gu, gv, gt = _proj_in(x, g_mix[0:1], w_main, w_gt, gb)

    h_ml = _mlstm(qk.reshape(B, S, 2 * d_ml), v.reshape(B, NC, CHUNK, d_ml), gt,
                  conv_w[0].reshape(CONV_WIDTH, 2 * d_ml), conv_b[0:1], g_head[0:1])
    h_ml = h_ml.reshape(T, d_ml)

    dh = D // N_XHEADS
    w_kv = w_kv_x[0]
    w_kt = jnp.transpose(w_kv[:, 0:D].reshape(D, N_XHEADS, dh), (1, 2, 0)).astype(BF16)
    w_v = jnp.transpose(w_kv[:, D:2 * D].reshape(D, N_XHEADS, dh), (1, 0, 2)).astype(BF16)
    kt, vm = _kv_proj(mem, g_mem[0:1], w_kt, w_v)

    w_r = jnp.concatenate([w_rg[0], w_re[0]], axis=1).T
    w_r = jnp.pad(w_r, ((0, LANES - w_r.shape[0]), (0, 0))).astype(BF16)
    b_r = jnp.concatenate([b_rg[0], b_re[0]])
    b_r = jnp.pad(b_r, (0, LANES - b_r.shape[0])).reshape(LANES, 1).astype(F32)
    bs_b = jnp.broadcast_to(b_s[0][:, :, None], (N_GROUPS, CHUNK, HEAD_DIM)).astype(F32)

    x2, hn, route, counts = _mix_attn(
        x.reshape(T, D), h_ml, o, gu, gv, ln_v_g[0:1], ln_v_b[0:1], w_s[0].astype(BF16), bs_b,
        w_out[0].astype(BF16), g_xattn[0:1], w_q_x[0].astype(BF16), kt, vm, w_o_x[0].astype(BF16),
        g_moe[0:1], w_r, b_r, S)

    blk_e, n_used, dd, n_slots, pad_start, pad_n = _route_tables(route, counts, T)
    x_disp = _dispatch(pad_start, pad_n, dd, hn, n_slots)

    y_disp = _experts(blk_e, n_used, x_disp, w_gate[0].astype(BF16), w_up[0].astype(BF16),
                      w_down[0].astype(BF16))

    out = _combine(dd, y_disp, x2, route, g_final.reshape(1, D))
    return out.reshape(B, S, D)
```

```python
import jax
import jax.numpy as jnp
from jax import lax
from jax.experimental import pallas as pl
from jax.experimental.pallas import tpu as pltpu

F32 = jnp.float32
BF16 = jnp.bfloat16

RMS_EPS = 1e-6
LN_EPS = 1e-5
NEG_INIT = -1e30

N_HEADS = 4
HEAD_DIM = 128
CHUNK = 128
CONV_WIDTH = 5
HALO = 8
N_GROUPS = 4
N_XHEADS = 4
N_EXPERT_GROUPS = 4
EXPERTS_PER_GROUP = 8
N_EXPERTS = N_EXPERT_GROUPS * EXPERTS_PER_GROUP
TOP_K = 2
LANES = 128

PROJ_TILE = 512
MIX_TILE = 512
MIX_SUB = 512
EXPERT_BLOCK = 256
COMBINE_TILE = 256
CONV_ROWS = 64
HEADS_PER_STEP = 2
MLSTM_UNROLL = 4
VMEM_LIMIT = 56 * 1024 * 1024


def _dot(a, b):
    return jnp.dot(a, b, preferred_element_type=F32)


def _dot_nt(a, b):
    return lax.dot_general(a, b, (((1,), (1,)), ((), ())), preferred_element_type=F32)


def _rms(x, g):
    return x * lax.rsqrt(jnp.mean(x * x, axis=-1, keepdims=True) + RMS_EPS) * g


def _sigmoid(x):
    return 1.0 / (1.0 + jnp.exp(-x))


def _silu(x):
    return x * _sigmoid(x)


def _gelu_tanh(x):
    c = 0.7978845608028654
    return 0.5 * x * (1.0 + jnp.tanh(c * (x + 0.044715 * (x * x * x))))


def _log_sigmoid(x):
    return jnp.minimum(x, 0.0) - jnp.log(1.0 + jnp.exp(-jnp.abs(x)))


TILE_ROWS = 8


def _store_row_tiles(ref, first_row, x):
    n = x.shape[0]
    for l in range(TILE_ROWS):
        ref[pl.ds(first_row * TILE_ROWS + l, n, stride=TILE_ROWS), :] = x[:, l * LANES:(l + 1) * LANES]


def _load_row_tiles(ref, first_row, n, lead=()):
    parts = [ref[lead + (pl.ds(first_row * TILE_ROWS + l, n, stride=TILE_ROWS), slice(None))]
             for l in range(TILE_ROWS)]
    return jnp.concatenate(parts, axis=1)


def _proj_in_kernel(x_ref, g_ref, w_ref, wgt_ref, gb_ref,
                    qk_ref, v_ref, o_ref, gu_ref, gv_ref, gt_ref):
    xb = _rms(x_ref[...], g_ref[...]).astype(BF16)
    qk_ref[...] = _dot(xb, w_ref[:, 0:1024])
    v_ref[...] = _dot(xb, w_ref[:, 1024:1536]).astype(BF16)
    o_ref[...] = _dot(xb, w_ref[:, 1536:2048])
    gu_ref[...] = _dot(xb, w_ref[:, 2048:2560])
    gv_ref[...] = _dot(xb, w_ref[:, 2560:3072])
    gt = _dot_nt(wgt_ref[...], xb) + gb_ref[...]
    for h in range(N_HEADS):
        for j in range(gt_ref.shape[1]):
            gt_ref[h, j] = gt[8 * h:8 * h + 8, CHUNK * j:CHUNK * (j + 1)]


def _proj_in(x, g_mix, w_main, w_gt, gate_b):
    B, S, D = x.shape
    tm = PROJ_TILE
    nj = tm // CHUNK
    NC = S // CHUNK
    T = B * S
    grid = (B, S // tm)
    row = lambda b, i: (b * (S // tm) + i, 0)
    const = lambda b, i: (0, 0)
    outs = pl.pallas_call(
        _proj_in_kernel,
        grid=grid,
        in_specs=[
            pl.BlockSpec((None, tm, D), lambda b, i: (b, i, 0)),
            pl.BlockSpec((1, D), const),
            pl.BlockSpec(w_main.shape, const),
            pl.BlockSpec(w_gt.shape, const),
            pl.BlockSpec(gate_b.shape, const),
        ],
        out_specs=[
            pl.BlockSpec((tm, 1024), row),
            pl.BlockSpec((tm, 512), row),
            pl.BlockSpec((tm, 512), row),
            pl.BlockSpec((tm, 512), row),
            pl.BlockSpec((tm, 512), row),
            pl.BlockSpec((None, N_HEADS, nj, 8, CHUNK), lambda b, i: (b, 0, i, 0, 0)),
        ],
        out_shape=[
            jax.ShapeDtypeStruct((T, 1024), F32),
            jax.ShapeDtypeStruct((T, 512), BF16),
            jax.ShapeDtypeStruct((T, 512), F32),
            jax.ShapeDtypeStruct((T, 512), F32),
            jax.ShapeDtypeStruct((T, 512), F32),
            jax.ShapeDtypeStruct((B, N_HEADS, NC, 8, CHUNK), F32),
        ],
        compiler_params=pltpu.CompilerParams(
            dimension_semantics=("parallel", "parallel"), vmem_limit_bytes=VMEM_LIMIT),
        name="proj_in",
    )(x, g_mix, w_main, w_gt, gate_b)
    return outs


def _conv_silu(src_ref, w, bias, r0, n_rows):
    S = src_ref.shape[0]
    acc = None
    for j in range(CONV_WIDTH):
        d = j - CONV_WIDTH // 2
        if r0 + d < 0 or r0 + d + n_rows > S:
            blk = src_ref[r0:r0 + n_rows, :]
            ridx = lax.broadcasted_iota(jnp.int32, blk.shape, 0)
            sh = pltpu.roll(blk, (-d) % n_rows, axis=0)
            tap = jnp.where((ridx + d >= 0) & (ridx + d < n_rows), sh, 0.0)
        else:
            tap = src_ref[r0 + d:r0 + d + n_rows, :]
        term = w[j:j + 1, :] * tap
        acc = term if acc is None else acc + term
    return _silu(acc + bias)


G_B, G_R, G_CM, G_WK, G_F, G_ML = range(6)


def _gate_rows(g, t_idx, s_idx):
    n = g.shape[0]
    ls = _log_sigmoid(g)
    hi = ls.astype(BF16).astype(F32)
    r1 = ls - hi
    mid = r1.astype(BF16).astype(F32)
    lo = (r1 - mid).astype(BF16).astype(F32)
    parts = jnp.concatenate([hi, mid, lo], axis=0).astype(BF16)
    one = lambda m: jnp.where(m, 1.0, 0.0).astype(BF16)
    rhs = jnp.concatenate([one(t_idx <= s_idx), one(t_idx >= s_idx), jnp.ones((CHUNK, CHUNK), BF16)], axis=1)
    cs = _dot(parts, rhs)
    cum = cs[0:n] + cs[n:2 * n] + cs[2 * n:3 * n]
    up1 = lambda a: pltpu.roll(a, n - 1, axis=0)
    fwd_row = lax.rem(lax.broadcasted_iota(jnp.int32, (n, CHUNK), 0), 8) == 0
    lane = lax.broadcasted_iota(jnp.int32, (n, CHUNK), 1)
    b = jnp.where(fwd_row, up1(cum[:, 0:CHUNK]), up1(cum[:, CHUNK:2 * CHUNK]))
    r = g - b
    f = up1(cum[:, 2 * CHUNK:3 * CHUNK])
    a = f + r
    ml = jnp.broadcast_to(jnp.max(a, axis=1, keepdims=True), a.shape)
    wk = jnp.exp(a - ml)
    pm = r
    sm = r
    s = 1
    while s < CHUNK:
        pm = jnp.maximum(pm, jnp.where(lane >= s, pltpu.roll(pm, s, axis=1), -jnp.inf))
        sm = jnp.maximum(sm, jnp.where(lane < CHUNK - s, pltpu.roll(sm, CHUNK - s, axis=1), -jnp.inf))
        s *= 2
    cm = jnp.where(fwd_row, pm, sm)
    return b, r, cm, wk, f, ml


def _mlstm_kernel(q_ref, k_ref, v_ref, g_ref, cwq_ref, cwk_ref, cbq_ref, cbk_ref, gh_ref,
                  out_ref, qc_ref, kt_ref, gate_ref, caug_ref, m_ref):
    NC = v_ref.shape[0]
    hps = HEADS_PER_STEP
    k_scale = HEAD_DIM ** -0.5

    cwq, cwk = cwq_ref[...], cwk_ref[...]
    cbq, cbk = cbq_ref[...], cbk_ref[...]
    n_sub = CHUNK // CONV_ROWS
    for c in range(NC):
        qb = jnp.concatenate([_conv_silu(q_ref, cwq, cbq, c * CHUNK + i * CONV_ROWS, CONV_ROWS)
                              for i in range(n_sub)], axis=0)
        kb = jnp.concatenate([_conv_silu(k_ref, cwk, cbk, c * CHUNK + i * CONV_ROWS, CONV_ROWS)
                              for i in range(n_sub)], axis=0) * k_scale
        for hd in range(hps):
            cols = slice(hd * HEAD_DIM, (hd + 1) * HEAD_DIM)
            qc_ref[hd, c] = qb[:, cols].astype(BF16)
            kt_ref[hd, c] = kb[:, cols].T

    t_idx = lax.broadcasted_iota(jnp.int32, (CHUNK, CHUNK), 0)
    s_idx = lax.broadcasted_iota(jnp.int32, (CHUNK, CHUNK), 1)
    ones = jnp.ones((CHUNK, HEAD_DIM), BF16)
    gh = gh_ref[...]

    for hd in range(hps):
        rows = _gate_rows(g_ref[hd].reshape(NC * 8, CHUNK), t_idx, s_idx)
        for kind, val in enumerate(rows):
            gate_ref[kind, hd] = val.reshape(NC, 8, CHUNK)

    def col_form(row):
        return jnp.broadcast_to(row, (CHUNK, CHUNK)).T

    def chunk_step(hd, c, direction):
        st = 2 * hd + direction
        row = slice(2 * direction, 2 * direction + 1)
        b_r = gate_ref[G_B, hd, c][row]
        r_r = gate_ref[G_R, hd, c][row]
        cm_r = gate_ref[G_CM, hd, c][row]
        wk_r = gate_ref[G_WK, hd, c][row]
        f_tot = gate_ref[G_F, hd, c][row]
        m_loc = gate_ref[G_ML, hd, c][row]
        causal = (s_idx <= t_idx) if direction == 0 else (s_idx >= t_idx)

        m_prev = m_ref[st][0:1]
        caug = caug_ref[st]
        q = qc_ref[hd, c]
        kt = kt_ref[hd, c]
        vaug = jnp.concatenate([v_ref[c, :, hd * HEAD_DIM:(hd + 1) * HEAD_DIM], ones], axis=1)

        b_c = col_form(b_r)
        mx_c = jnp.maximum(col_form(cm_r), m_prev)
        w_mat = jnp.where(causal, jnp.exp(r_r - mx_c), 0.0)
        s_inter = jnp.exp(m_prev - mx_c)
        scores = _dot(q, kt.astype(BF16))
        wqk = (w_mat * scores).astype(BF16)
        intra = _dot(wqk, vaug)
        carried = _dot(q, caug.astype(BF16))
        num = intra[:, 0:HEAD_DIM] + s_inter * carried[:, 0:HEAD_DIM]
        den = intra[:, HEAD_DIM:2 * HEAD_DIM] + s_inter * carried[:, HEAD_DIM:2 * HEAD_DIM]
        h = num / jnp.maximum(jnp.abs(den), jnp.exp(-(b_c + mx_c)))

        kw = (kt * wk_r).astype(BF16)
        c_loc = _dot(kw, vaug)
        m_new = jnp.maximum(f_tot + m_prev, m_loc)
        s_old = jnp.exp(f_tot + m_prev - m_new)
        s_new = jnp.exp(m_loc - m_new)
        wide = lambda a: jnp.concatenate([a, a], axis=1)
        caug_ref[st] = wide(s_old) * caug + wide(s_new) * c_loc
        m_ref[st] = jnp.broadcast_to(m_new, m_ref.shape[1:])
        return h

    caug_ref[...] = jnp.zeros_like(caug_ref)
    m_ref[...] = jnp.full(m_ref.shape, NEG_INIT, F32)

    def finish(hd, hs):
        cols = slice(hd * HEAD_DIM, (hd + 1) * HEAD_DIM)
        return hs * lax.rsqrt(jnp.mean(hs * hs, axis=-1, keepdims=True) + RMS_EPS) * gh[:, cols]

    def make_body(final):
        def body(i, carry):
            cf = i
            cb = NC - 1 - i
            for hd in range(hps):
                cols = slice(hd * HEAD_DIM, (hd + 1) * HEAD_DIM)
                h_f = chunk_step(hd, cf, 0)
                h_b = chunk_step(hd, cb, 1)
                if final:
                    out_ref[cf, :, cols] = finish(hd, out_ref[cf, :, cols] + h_f)
                    out_ref[cb, :, cols] = finish(hd, out_ref[cb, :, cols] + h_b)
                else:
                    out_ref[cf, :, cols] = h_f
                    out_ref[cb, :, cols] = h_b
            return carry
        return body

    lax.fori_loop(0, NC // 2, make_body(False), 0, unroll=MLSTM_UNROLL)
    lax.fori_loop(NC // 2, NC, make_body(True), 0, unroll=MLSTM_UNROLL)


def _mlstm(qk, v, gt, conv_w, conv_b, g_head):
    B, S = qk.shape[0], qk.shape[1]
    NC = S // CHUNK
    assert NC % 2 == 0
    hps = HEADS_PER_STEP
    w = hps * HEAD_DIM
    k_off = N_HEADS // hps
    seq = lambda off: pl.BlockSpec((None, S, w), lambda b, p: (b, 0, p + off))
    blk = lambda off: pl.BlockSpec((None, NC, CHUNK, w), lambda b, p: (b, 0, 0, p + off))
    par = lambda r, off: pl.BlockSpec((r, w), lambda b, p: (0, p + off))
    return pl.pallas_call(
        _mlstm_kernel,
        grid=(B, N_HEADS // hps),
        in_specs=[
            seq(0), seq(k_off), blk(0),
            pl.BlockSpec((None, hps, NC, 8, CHUNK), lambda b, p: (b, p, 0, 0, 0)),
            par(CONV_WIDTH, 0), par(CONV_WIDTH, k_off), par(1, 0), par(1, k_off), par(1, 0),
        ],
        out_specs=blk(0),
        out_shape=jax.ShapeDtypeStruct(v.shape, F32),
        scratch_shapes=[
            pltpu.VMEM((hps, NC, CHUNK, HEAD_DIM), BF16),
            pltpu.VMEM((hps, NC, HEAD_DIM, CHUNK), F32),
            pltpu.VMEM((6, hps, NC, 8, CHUNK), F32),
            pltpu.VMEM((2 * hps, HEAD_DIM, 2 * HEAD_DIM), F32),
            pltpu.VMEM((2 * hps, 8, LANES), F32),
        ],
        compiler_params=pltpu.CompilerParams(
            dimension_semantics=("parallel", "parallel"), vmem_limit_bytes=VMEM_LIMIT),
        name="mlstm",
    )(qk, qk, v, gt, conv_w, conv_w, conv_b, conv_b, g_head)


def _kv_kernel(mem_ref, g_ref, wkt_ref, wv_ref, kt_ref, v_ref):
    mn = _rms(mem_ref[...], g_ref[...]).astype(BF16)
    for h in range(N_XHEADS):
        kt_ref[h] = _dot_nt(wkt_ref[h], mn).astype(BF16)
        v_ref[h] = _dot(mn, wv_ref[h]).astype(BF16)


def _kv_proj(mem, g_mem, w_kt, w_v):
    B, M, D = mem.shape
    dh = D // N_XHEADS
    return pl.pallas_call(
        _kv_kernel,
        grid=(B,),
        in_specs=[
            pl.BlockSpec((None, M, D), lambda b: (b, 0, 0)),
            pl.BlockSpec((1, D), lambda b: (0, 0)),
            pl.BlockSpec(w_kt.shape, lambda b: (0, 0, 0)),
            pl.BlockSpec(w_v.shape, lambda b: (0, 0, 0)),
        ],
        out_specs=[
            pl.BlockSpec((None, N_XHEADS, dh, M), lambda b: (b, 0, 0, 0)),
            pl.BlockSpec((None, N_XHEADS, M, dh), lambda b: (b, 0, 0, 0)),
        ],
        out_shape=[
            jax.ShapeDtypeStruct((B, N_XHEADS, dh, M), BF16),
            jax.ShapeDtypeStruct((B, N_XHEADS, M, dh), BF16),
        ],
        compiler_params=pltpu.CompilerParams(
            dimension_semantics=("parallel",), vmem_limit_bytes=VMEM_LIMIT),
        name="kv_proj",
    )(mem, g_mem, w_kt, w_v)


def _mix_rows(r0, n, x_ref, hm_ref, o_ref, gu_ref, gv_ref, lng_ref, lnb_ref, ws_ref, bs_ref, wout_ref,
              gx_ref, wq_ref, kt_ref, vm_ref, wo_ref, gmoe_ref, wr_ref, br_ref, x2_ref, hn_ref):
    rs = slice(r0, r0 + n)
    gd = HEAD_DIM
    y_mlstm = _sigmoid(o_ref[rs, :]) * hm_ref[rs, :]
    gu = _gelu_tanh(gu_ref[rs, :])
    gv = _gelu_tanh(gv_ref[rs, :])
    mu = jnp.mean(gv, axis=-1, keepdims=True)
    gc = gv - mu
    gvn = gc * lax.rsqrt(jnp.mean(gc * gc, axis=-1, keepdims=True) + LN_EPS) * lng_ref[...] + lnb_ref[...]
    gvb = gvn.astype(BF16)
    rows = []
    for j in range(n // CHUNK):
        cols = []
        for g in range(N_GROUPS):
            sp = _dot(ws_ref[g], gvb[j * CHUNK:(j + 1) * CHUNK, g * gd:(g + 1) * gd]) + bs_ref[g]
            cols.append(sp)
        rows.append(jnp.concatenate(cols, axis=1))
    y_gmlp = gu * jnp.concatenate(rows, axis=0)
    n_ml = hm_ref.shape[1]
    mix = _dot(y_mlstm.astype(BF16), wout_ref[0:n_ml, :]) + _dot(y_gmlp.astype(BF16), wout_ref[n_ml:, :])
    x1 = x_ref[rs, :] + mix

    hq = _rms(x1, gx_ref[...]).astype(BF16)
    q = _dot(hq, wq_ref[...])
    dh = kt_ref.shape[1]
    scale = dh ** -0.5
    heads = []
    for h in range(N_XHEADS):
        s = _dot(q[:, h * dh:(h + 1) * dh].astype(BF16), kt_ref[h]) * scale
        e = jnp.exp(s - jnp.max(s, axis=-1, keepdims=True))
        p = e / jnp.sum(e, axis=-1, keepdims=True)
        heads.append(_dot(p.astype(BF16), vm_ref[h]))
    att = jnp.concatenate(heads, axis=1).astype(BF16)
    x2 = x1 + _dot(att, wo_ref[...])
    x2_ref[rs, :] = x2

    hn = _rms(x2, gmoe_ref[...])
    _store_row_tiles(hn_ref, r0, hn)
    return _dot_nt(wr_ref[...], hn.astype(BF16)) + br_ref[...]


def _mix_attn_kernel(x_ref, hm_ref, o_ref, gu_ref, gv_ref, lng_ref, lnb_ref, ws_ref, bs_ref, wout_ref,
                     gx_ref, wq_ref, kt_ref, vm_ref, wo_ref, gmoe_ref, wr_ref, br_ref, utri_ref,
                     x2_ref, hn_ref, rt_ref, cnt_ref, run_ref):
    tm = x_ref.shape[0]
    parts = [_mix_rows(r0, MIX_SUB, x_ref, hm_ref, o_ref, gu_ref, gv_ref, lng_ref, lnb_ref, ws_ref, bs_ref,
                       wout_ref, gx_ref, wq_ref, kt_ref, vm_ref, wo_ref, gmoe_ref, wr_ref, br_ref,
                       x2_ref, hn_ref)
             for r0 in range(0, tm, MIX_SUB)]
    n_log = N_EXPERT_GROUPS + N_EXPERTS + 4
    lg = jnp.concatenate(parts, axis=1)[0:n_log]

    rowi = lax.broadcasted_iota(jnp.int32, lg.shape, 0).astype(F32)
    none = float(LANES)
    gmask = rowi < N_EXPERT_GROUPS
    gmax = jnp.max(jnp.where(gmask, lg, -jnp.inf), axis=0, keepdims=True)
    ge = jnp.where(gmask, jnp.exp(lg - gmax), 0.0)
    p = ge / jnp.sum(ge, axis=0, keepdims=True)
    p_top = jnp.max(p, axis=0, keepdims=True)
    g_idx = jnp.min(jnp.where(jnp.where(gmask, p, -1.0) == p_top, rowi, none), axis=0, keepdims=True)
    lo = N_EXPERT_GROUPS + EXPERTS_PER_GROUP * g_idx
    el = jnp.where(rowi >= lo, jnp.where(rowi < lo + EXPERTS_PER_GROUP, lg, -jnp.inf), -jnp.inf)
    v1 = jnp.max(el, axis=0, keepdims=True)
    i1 = jnp.min(jnp.where(el == v1, rowi, none), axis=0, keepdims=True)
    el2 = jnp.where(rowi == i1, -jnp.inf, el)
    v2 = jnp.max(el2, axis=0, keepdims=True)
    i2 = jnp.min(jnp.where(el2 == v2, rowi, none), axis=0, keepdims=True)
    ex2 = jnp.exp(v2 - v1)
    den = 1.0 + ex2
    w1 = p_top * (1.0 / den)
    w2 = p_top * (ex2 / den)
    e1 = i1 - N_EXPERT_GROUPS
    e2 = i2 - N_EXPERT_GROUPS

    @pl.when(pl.program_id(0) == 0)
    def _():
        run_ref[...] = jnp.zeros_like(run_ref)

    erow = lax.broadcasted_iota(jnp.int32, (N_EXPERTS, tm), 0).astype(F32)
    is1 = erow == e1
    is2 = erow == e2
    oh = jnp.where(is1, 1.0, jnp.where(is2, 1.0, 0.0))
    run = run_ref[...]
    before = _dot(oh.astype(BF16), utri_ref[...]) + jnp.concatenate([run] * (tm // LANES), axis=1)
    rank1 = jnp.sum(jnp.where(is1, before, 0.0), axis=0, keepdims=True)
    rank2 = jnp.sum(jnp.where(is2, before, 0.0), axis=0, keepdims=True)
    total = run + jnp.sum(oh, axis=1, keepdims=True)
    run_ref[...] = total
    cnt_ref[...] = total
    zero = jnp.zeros_like(w1)
    rt_ref[...] = jnp.concatenate([w1, w2, e1, e2, rank1, rank2, zero, zero], axis=0)


def _mix_attn(x, hm, o, gu, gv, ln_g, ln_b, w_s, b_s, w_out, g_x, w_q, kt, vm, w_o, g_moe, w_r, b_r, S):
    T, D = x.shape
    tm = MIX_TILE
    per_b = S // tm
    row = lambda i: (i, 0)
    c2 = lambda i: (0, 0)
    c3 = lambda i: (0, 0, 0)
    full = lambda a: pl.BlockSpec(a.shape, c2 if a.ndim == 2 else c3)
    tile = lambda a: pl.BlockSpec((tm, a.shape[1]), row)
    ti = lax.broadcasted_iota(jnp.int32, (tm, tm), 0)
    si = lax.broadcasted_iota(jnp.int32, (tm, tm), 1)
    utri = jnp.where(ti < si, 1.0, 0.0).astype(BF16)
    return pl.pallas_call(
        _mix_attn_kernel,
        grid=(T // tm,),
        in_specs=[
            tile(x), tile(hm), tile(o), tile(gu), tile(gv),
            full(ln_g), full(ln_b), full(w_s), full(b_s), full(w_out),
            full(g_x), full(w_q),
            pl.BlockSpec((None,) + kt.shape[1:], lambda i: (i // per_b, 0, 0, 0)),
            pl.BlockSpec((None,) + vm.shape[1:], lambda i: (i // per_b, 0, 0, 0)),
            full(w_o), full(g_moe), full(w_r), full(b_r), full(utri),
        ],
        out_specs=[pl.BlockSpec((tm, D), row), pl.BlockSpec((tm * TILE_ROWS, LANES), row),
                   pl.BlockSpec((8, tm), lambda i: (0, i)),
                   pl.BlockSpec((N_EXPERTS, LANES), c2)],
        out_shape=[
            jax.ShapeDtypeStruct((T, D), F32),
            jax.ShapeDtypeStruct((T * TILE_ROWS, LANES), F32),
            jax.ShapeDtypeStruct((8, T), F32),
            jax.ShapeDtypeStruct((N_EXPERTS, LANES), F32),
        ],
        scratch_shapes=[pltpu.VMEM((N_EXPERTS, LANES), F32)],
        compiler_params=pltpu.CompilerParams(
            dimension_semantics=("arbitrary",), vmem_limit_bytes=VMEM_LIMIT),
        name="mix_attn",
    )(x, hm, o, gu, gv, ln_g, ln_b, w_s, b_s, w_out, g_x, w_q, kt, vm, w_o, g_moe, w_r, b_r, utri)


def _idx_copy(idx_hbm, block, idx_smem, slot, isem):
    return pltpu.make_async_copy(idx_hbm.at[block], idx_smem.at[slot], isem.at[slot])


def _dispatch_kernel(pad_start_ref, pad_n_ref, dd_hbm, h_ref, xd_hbm, idx, zrow, sem, isem, zsem):
    i = pl.program_id(0)
    nb = pl.num_programs(0)
    R = h_ref.shape[0] // TILE_ROWS
    slot = lax.rem(i, 2)

    def tile_at(ref, first):
        return ref.at[pl.ds(pl.multiple_of(first, TILE_ROWS), TILE_ROWS), :]

    def row_copy(r, dst8):
        return pltpu.make_async_copy(tile_at(h_ref, r * TILE_ROWS), tile_at(xd_hbm, dst8), sem)

    def zero_copy(dst):
        return pltpu.make_async_copy(zrow, tile_at(xd_hbm, dst * TILE_ROWS), zsem)

    @pl.when(i == 0)
    def _():
        first = _idx_copy(dd_hbm, 0, idx, 0, isem)
        first.start()
        zrow[...] = jnp.zeros_like(zrow)
        for e in range(pad_start_ref.shape[0]):
            base = pad_start_ref[e]
            count = pad_n_ref[e]

            def start_body(j, carry, base=base):
                zero_copy(base + j).start()
                return carry

            def wait_body(j, carry):
                zero_copy(0).wait()
                return carry

            lax.fori_loop(0, count, start_body, 0)
            lax.fori_loop(0, count, wait_body, 0)
        first.wait()

    @pl.when(i + 1 < nb)
    def _():
        _idx_copy(dd_hbm, i + 1, idx, 1 - slot, isem).start()

    for s in range(2):
        @pl.when(slot == s)
        def _(s=s):
            for r in range(R):
                for k in range(TOP_K):
                    row_copy(r, idx[s, k * R + r]).start()
    for r in range(TOP_K * R):
        row_copy(0, 0).wait()

    @pl.when(i + 1 < nb)
    def _():
        _idx_copy(dd_hbm, i + 1, idx, 1 - slot, isem).wait()


def _dispatch(pad_start, pad_n, dd8, hn, n_slots):
    nt, two_r = dd8.shape
    R = two_r // TOP_K
    any_spec = pl.BlockSpec(memory_space=pl.ANY)
    return pl.pallas_call(
        _dispatch_kernel,
        grid_spec=pltpu.PrefetchScalarGridSpec(
            num_scalar_prefetch=2,
            grid=(nt,),
            in_specs=[any_spec, pl.BlockSpec((R * TILE_ROWS, LANES), lambda i, ps, pn: (i, 0))],
            out_specs=any_spec,
            scratch_shapes=[pltpu.SMEM((2, two_r), jnp.int32), pltpu.VMEM((TILE_ROWS, LANES), hn.dtype),
                            pltpu.SemaphoreType.DMA(()), pltpu.SemaphoreType.DMA((2,)),
                            pltpu.SemaphoreType.DMA(())],
        ),
        out_shape=jax.ShapeDtypeStruct((n_slots * TILE_ROWS, LANES), hn.dtype),
        compiler_params=pltpu.CompilerParams(
            dimension_semantics=("arbitrary",), vmem_limit_bytes=VMEM_LIMIT),
        name="dispatch",
    )(pad_start, pad_n, dd8, hn)


def _expert_kernel(blk_e_ref, n_used_ref, x_ref, wg_ref, wu_ref, wd_ref, y_ref):
    used = pl.program_id(0) < n_used_ref[0]

    @pl.when(used)
    def _():
        R = x_ref.shape[0] // TILE_ROWS
        xb = _load_row_tiles(x_ref, 0, R).astype(BF16)
        act = (_silu(_dot(xb, wg_ref[...])) * _dot(xb, wu_ref[...])).astype(BF16)
        _store_row_tiles(y_ref, 0, _dot(act, wd_ref[...]))

    @pl.when(jnp.logical_not(used))
    def _():
        y_ref[...] = jnp.zeros_like(y_ref)


def _experts(blk_e, n_used, x_disp, w_g, w_u, w_d):
    R = EXPERT_BLOCK
    D = w_g.shape[1]
    P = x_disp.shape[0] // TILE_ROWS
    nb = P // R
    de = w_d.shape[1]
    tiles = (R * TILE_ROWS, LANES)
    last = lambda i, nu: jnp.minimum(i, nu[0] - 1)
    return pl.pallas_call(
        _expert_kernel,
        grid_spec=pltpu.PrefetchScalarGridSpec(
            num_scalar_prefetch=2,
            grid=(nb,),
            in_specs=[
                pl.BlockSpec(tiles, lambda i, be, nu: (last(i, nu), 0)),
                pl.BlockSpec((None, D, de), lambda i, be, nu: (be[last(i, nu)], 0, 0)),
                pl.BlockSpec((None, D, de), lambda i, be, nu: (be[last(i, nu)], 0, 0)),
                pl.BlockSpec((None, de, D), lambda i, be, nu: (be[last(i, nu)], 0, 0)),
            ],
            out_specs=pl.BlockSpec(tiles, lambda i, be, nu: (i, 0)),
        ),
        out_shape=jax.ShapeDtypeStruct((P * TILE_ROWS, LANES), F32),
        compiler_params=pltpu.CompilerParams(
            dimension_semantics=("arbitrary",), vmem_limit_bytes=VMEM_LIMIT),
        name="experts",
    )(blk_e, n_used, x_disp, w_g, w_u, w_d)


def _combine_kernel(dd_hbm, y_hbm, x2_ref, rt_ref, gf_ref, out_ref, buf, idx, sem, isem):
    i = pl.program_id(0)
    nb = pl.num_programs(0)
    R = x2_ref.shape[0]
    slot = lax.rem(i, 2)

    def row_copy(src8, s, r):
        return pltpu.make_async_copy(y_hbm.at[pl.ds(pl.multiple_of(src8, TILE_ROWS), TILE_ROWS), :],
                                     buf.at[s, pl.ds(r * TILE_ROWS, TILE_ROWS), :], sem.at[s])

    def issue_rows(s):
        for r in range(TOP_K * R):
            row_copy(idx[s, r], s, r).start()

    @pl.when(i == 0)
    def _():
        first = _idx_copy(dd_hbm, 0, idx, 0, isem)
        first.start()
        first.wait()
        issue_rows(0)

        @pl.when(nb > 1)
        def _():
            _idx_copy(dd_hbm, 1, idx, 1, isem).start()

    for s in range(2):
        @pl.when((i + 1 < nb) & (slot == 1 - s))
        def _(s=s):
            _idx_copy(dd_hbm, i + 1, idx, s, isem).wait()
            issue_rows(s)

    @pl.when(i + 2 < nb)
    def _():
        _idx_copy(dd_hbm, i + 2, idx, slot, isem).start()

    for r in range(TOP_K * R):
        row_copy(0, slot, r).wait()

    rt = rt_ref[...]
    D = x2_ref.shape[1]
    for j in range(R // LANES):
        rows = slice(j * LANES, (j + 1) * LANES)
        wa = jnp.broadcast_to(rt[0:1, rows], (LANES, LANES)).T
        wb = jnp.broadcast_to(rt[1:2, rows], (LANES, LANES)).T
        wa = jnp.concatenate([wa] * (D // LANES), axis=1)
        wb = jnp.concatenate([wb] * (D // LANES), axis=1)
        ya = _load_row_tiles(buf, j * LANES, LANES, lead=(slot,))
        yb = _load_row_tiles(buf, R + j * LANES, LANES, lead=(slot,))
        out_ref[rows, :] = _rms(x2_ref[rows, :] + (wa * ya + wb * yb), gf_ref[...])


def _combine(dd, y_disp, x2, route, g_final):
    T, D = x2.shape
    R = COMBINE_TILE
    row = lambda i: (i, 0)
    return pl.pallas_call(
        _combine_kernel,
        grid=(T // R,),
        in_specs=[
            pl.BlockSpec(memory_space=pl.ANY),
            pl.BlockSpec(memory_space=pl.ANY),
            pl.BlockSpec((R, D), row),
            pl.BlockSpec((8, R), lambda i: (0, i)),
            pl.BlockSpec((1, D), lambda i: (0, 0)),
        ],
        out_specs=pl.BlockSpec((R, D), row),
        scratch_shapes=[pltpu.VMEM((2, TOP_K * R * TILE_ROWS, LANES), F32), pltpu.SMEM((2, TOP_K * R), jnp.int32),
                        pltpu.SemaphoreType.DMA((2,)), pltpu.SemaphoreType.DMA((2,))],
        out_shape=jax.ShapeDtypeStruct((T, D), F32),
        compiler_params=pltpu.CompilerParams(
            dimension_semantics=("arbitrary",), vmem_limit_bytes=VMEM_LIMIT),
        name="combine",
    )(dd, y_disp, x2, route, g_final)


def _route_tables(route, counts_row, T):
    R = EXPERT_BLOCK
    counts = counts_row[:, 0].astype(jnp.int32)
    padded = ((counts + R - 1) // R) * R
    pend = jnp.cumsum(padded)
    pstart = pend - padded
    nb = (T * TOP_K) // R + N_EXPERTS
    first_slot = jnp.arange(nb, dtype=jnp.int32) * R
    blk_e = jnp.sum((pend[None, :] <= first_slot[:, None]).astype(jnp.int32), axis=1)
    blk_e = jnp.minimum(blk_e, N_EXPERTS - 1)
    n_used = (pend[N_EXPERTS - 1:N_EXPERTS] // R).astype(jnp.int32)
    e = route[2:4].astype(jnp.int32)
    rank = route[4:6].astype(jnp.int32)
    sel = e[None, :, :] == jnp.arange(N_EXPERTS, dtype=jnp.int32)[:, None, None]
    dest = jnp.sum(jnp.where(sel, pstart[:, None, None], 0), axis=0) + rank
    d = (dest * TILE_ROWS).reshape(TOP_K, T // COMBINE_TILE, COMBINE_TILE)
    dd = jnp.transpose(d, (1, 0, 2)).reshape(T // COMBINE_TILE, TOP_K * COMBINE_TILE)
    total = pend[N_EXPERTS - 1:N_EXPERTS]
    pad_start = jnp.concatenate([pstart + counts, total])
    pad_n = jnp.concatenate([padded - counts, nb * R - total])
    return blk_e, n_used, dd, nb * R, pad_start, pad_n


def kernel(x, mem, g_mix, w_in, conv_w, conv_b, gate_b, g_head, ln_v_g, ln_v_b, w_s, b_s, w_out,
           g_xattn, g_mem, w_q_x, w_kv_x, w_o_x, g_moe, w_rg, b_rg, w_re, b_re,
           w_gate, w_up, w_down, g_final):
    B, S, D = x.shape
    T = B * S
    NC = S // CHUNK
    d_ml = N_HEADS * HEAD_DIM
    n_gates = 4 * N_HEADS
    assert w_in.shape[0] == 1, "one layer"
    assert S % PROJ_TILE == 0 and S % MIX_TILE == 0 and T % COMBINE_TILE == 0
    assert (T * TOP_K) % EXPERT_BLOCK == 0

    w = w_in[0]
    c_g = 4 * d_ml
    w_main = jnp.concatenate([w[:, 0:c_g], w[:, c_g + n_gates:]], axis=1).astype(BF16)
    wg = w[:, c_g:c_g + n_gates].reshape(D, 4, N_HEADS)
    wg = jnp.pad(jnp.transpose(wg, (2, 1, 0)), ((0, 0), (0, 4), (0, 0)))
    w_gt = wg.reshape(N_HEADS * 8, D).astype(BF16)
    gb = jnp.pad(jnp.transpose(gate_b[0].reshape(4, N_HEADS), (1, 0)), ((0, 0), (0, 4)))
    gb = gb.reshape(N_HEADS * 8, 1).astype(F32)

    qk, v, o, gu, gv, gt = _proj_in(x, g_mix[0:1], w_main, w_gt, gb)

    h_ml = _mlstm(qk.reshape(B, S, 2 * d_ml), v.reshape(B, NC, CHUNK, d_ml), gt,
                  conv_w[0].reshape(CONV_WIDTH, 2 * d_ml), conv_b[0:1], g_head[0:1])
    h_ml = h_ml.reshape(T, d_ml)

    dh = D // N_XHEADS
    w_kv = w_kv_x[0]
    w_kt = jnp.transpose(w_kv[:, 0:D].reshape(D, N_XHEADS, dh), (1, 2, 0)).astype(BF16)
    w_v = jnp.transpose(w_kv[:, D:2 * D].reshape(D, N_XHEADS, dh), (1, 0, 2)).astype(BF16)
    kt, vm = _kv_proj(mem, g_mem[0:1], w_kt, w_v)

    w_r = jnp.concatenate([w_rg[0], w_re[0]], axis=1).T
    w_r = jnp.pad(w_r, ((0, LANES - w_r.shape[0]), (0, 0))).astype(BF16)
    b_r = jnp.concatenate([b_rg[0], b_re[0]])
    b_r = jnp.pad(b_r, (0, LANES - b_r.shape[0])).reshape(LANES, 1).astype(F32)
    bs_b = jnp.broadcast_to(b_s[0][:, :, None], (N_GROUPS, CHUNK, HEAD_DIM)).astype(F32)

    x2, hn, route, counts = _mix_attn(
        x.reshape(T, D), h_ml, o, gu, gv, ln_v_g[0:1], ln_v_b[0:1], w_s[0].astype(BF16), bs_b,
        w_out[0].astype(BF16), g_xattn[0:1], w_q_x[0].astype(BF16), kt, vm, w_o_x[0].astype(BF16),
        g_moe[0:1], w_r, b_r, S)

    blk_e, n_used, dd, n_slots, pad_start, pad_n = _route_tables(route, counts, T)
    x_disp = _dispatch(pad_start, pad_n, dd, hn, n_slots)

    y_disp = _experts(blk_e, n_used, x_disp, w_gate[0].astype(BF16), w_up[0].astype(BF16),
                      w_down[0].astype(BF16))

    out = _combine(dd, y_disp, x2, route, g_final.reshape(1, D))
    return out.reshape(B, S, D)
```

```python
import jax
import jax.numpy as jnp
from jax import lax
from jax.experimental import pallas as pl
from jax.experimental.pallas import tpu as pltpu

F32 = jnp.float32
BF16 = jnp.bfloat16

RMS_EPS = 1e-6
LN_EPS = 1e-5
NEG_INIT = -1e30

N_HEADS = 4
HEAD_DIM = 128
CHUNK = 128
CONV_WIDTH = 5
HALO = 8
N_GROUPS = 4
N_XHEADS = 4
N_EXPERT_GROUPS = 4
EXPERTS_PER_GROUP = 8
N_EXPERTS = N_EXPERT_GROUPS * EXPERTS_PER_GROUP
TOP_K = 2
LANES = 128

PROJ_TILE = 512
MIX_TILE = 512
MIX_SUB = 512
EXPERT_BLOCK = 256
COMBINE_TILE = 256
CONV_ROWS = 64
HEADS_PER_STEP = 2
MLSTM_UNROLL = 4
VMEM_LIMIT = 56 * 1024 * 1024


def _dot(a, b):
    return jnp.dot(a, b, preferred_element_type=F32)


def _dot_nt(a, b):
    return lax.dot_general(a, b, (((1,), (1,)), ((), ())), preferred_element_type=F32)


def _rms(x, g):
    return x * lax.rsqrt(jnp.mean(x * x, axis=-1, keepdims=True) + RMS_EPS) * g


def _sigmoid(x):
    return 1.0 / (1.0 + jnp.exp(-x))


def _silu(x):
    return x * _sigmoid(x)


def _gelu_tanh(x):
    c = 0.7978845608028654
    return 0.5 * x * (1.0 + jnp.tanh(c * (x + 0.044715 * (x * x * x))))


def _log_sigmoid(x):
    return jnp.minimum(x, 0.0) - jnp.log(1.0 + jnp.exp(-jnp.abs(x)))


TILE_ROWS = 8


def _store_row_tiles(ref, first_row, x):
    n = x.shape[0]
    for l in range(TILE_ROWS):
        ref[pl.ds(first_row * TILE_ROWS + l, n, stride=TILE_ROWS), :] = x[:, l * LANES:(l + 1) * LANES]


def _load_row_tiles(ref, first_row, n, lead=()):
    parts = [ref[lead + (pl.ds(first_row * TILE_ROWS + l, n, stride=TILE_ROWS), slice(None))]
             for l in range(TILE_ROWS)]
    return jnp.concatenate(parts, axis=1)


def _proj_in_kernel(x_ref, g_ref, w_ref, wgt_ref, gb_ref,
                    qk_ref, v_ref, o_ref, gu_ref, gv_ref, gt_ref):
    xb = _rms(x_ref[...], g_ref[...]).astype(BF16)
    qk_ref[...] = _dot(xb, w_ref[:, 0:1024])
    v_ref[...] = _dot(xb, w_ref[:, 1024:1536]).astype(BF16)
    o_ref[...] = _dot(xb, w_ref[:, 1536:2048])
    gu_ref[...] = _dot(xb, w_ref[:, 2048:2560])
    gv_ref[...] = _dot(xb, w_ref[:, 2560:3072])
    gt = _dot_nt(wgt_ref[...], xb) + gb_ref[...]
    for h in range(N_HEADS):
        for j in range(gt_ref.shape[1]):
            gt_ref[h, j] = gt[8 * h:8 * h + 8, CHUNK * j:CHUNK * (j + 1)]


def _proj_in(x, g_mix, w_main, w_gt, gate_b):
    B, S, D = x.shape
    tm = PROJ_TILE
    nj = tm // CHUNK
    NC = S // CHUNK
    T = B * S
    grid = (B, S // tm)
    row = lambda b, i: (b * (S // tm) + i, 0)
    const = lambda b, i: (0, 0)
    outs = pl.pallas_call(
        _proj_in_kernel,
        grid=grid,
        in_specs=[
            pl.BlockSpec((None, tm, D), lambda b, i: (b, i, 0)),
            pl.BlockSpec((1, D), const),
            pl.BlockSpec(w_main.shape, const),
            pl.BlockSpec(w_gt.shape, const),
            pl.BlockSpec(gate_b.shape, const),
        ],
        out_specs=[
            pl.BlockSpec((tm, 1024), row),
            pl.BlockSpec((tm, 512), row),
            pl.BlockSpec((tm, 512), row),
            pl.BlockSpec((tm, 512), row),
            pl.BlockSpec((tm, 512), row),
            pl.BlockSpec((None, N_HEADS, nj, 8, CHUNK), lambda b, i: (b, 0, i, 0, 0)),
        ],
        out_shape=[
            jax.ShapeDtypeStruct((T, 1024), F32),
            jax.ShapeDtypeStruct((T, 512), BF16),
            jax.ShapeDtypeStruct((T, 512), F32),
            jax.ShapeDtypeStruct((T, 512), F32),
            jax.ShapeDtypeStruct((T, 512), F32),
            jax.ShapeDtypeStruct((B, N_HEADS, NC, 8, CHUNK), F32),
        ],
        compiler_params=pltpu.CompilerParams(
            dimension_semantics=("parallel", "parallel"), vmem_limit_bytes=VMEM_LIMIT),
        name="proj_in",
    )(x, g_mix, w_main, w_gt, gate_b)
    return outs


def _conv_silu(src_ref, w, bias, r0, n_rows):
    S = src_ref.shape[0]
    acc = None
    for j in range(CONV_WIDTH):
        d = j - CONV_WIDTH // 2
        if r0 + d < 0 or r0 + d + n_rows > S:
            blk = src_ref[r0:r0 + n_rows, :]
            ridx = lax.broadcasted_iota(jnp.int32, blk.shape, 0)
            sh = pltpu.roll(blk, (-d) % n_rows, axis=0)
            tap = jnp.where((ridx + d >= 0) & (ridx + d < n_rows), sh, 0.0)
        else:
            tap = src_ref[r0 + d:r0 + d + n_rows, :]
        term = w[j:j + 1, :] * tap
        acc = term if acc is None else acc + term
    return _silu(acc + bias)


G_B, G_R, G_CM, G_WK, G_F, G_ML = range(6)


def _gate_rows(g, t_idx, s_idx):
    n = g.shape[0]
    ls = _log_sigmoid(g)
    hi = ls.astype(BF16).astype(F32)
    r1 = ls - hi
    mid = r1.astype(BF16).astype(F32)
    lo = (r1 - mid).astype(BF16).astype(F32)
    parts = jnp.concatenate([hi, mid, lo], axis=0).astype(BF16)
    one = lambda m: jnp.where(m, 1.0, 0.0).astype(BF16)
    rhs = jnp.concatenate([one(t_idx <= s_idx), one(t_idx >= s_idx), jnp.ones((CHUNK, CHUNK), BF16)], axis=1)
    cs = _dot(parts, rhs)
    cum = cs[0:n] + cs[n:2 * n] + cs[2 * n:3 * n]
    up1 = lambda a: pltpu.roll(a, n - 1, axis=0)
    fwd_row = lax.rem(lax.broadcasted_iota(jnp.int32, (n, CHUNK), 0), 8) == 0
    lane = lax.broadcasted_iota(jnp.int32, (n, CHUNK), 1)
    b = jnp.where(fwd_row, up1(cum[:, 0:CHUNK]), up1(cum[:, CHUNK:2 * CHUNK]))
    r = g - b
    f = up1(cum[:, 2 * CHUNK:3 * CHUNK])
    a = f + r
    ml = jnp.broadcast_to(jnp.max(a, axis=1, keepdims=True), a.shape)
    wk = jnp.exp(a - ml)
    pm = r
    sm = r
    s = 1
    while s < CHUNK:
        pm = jnp.maximum(pm, jnp.where(lane >= s, pltpu.roll(pm, s, axis=1), -jnp.inf))
        sm = jnp.maximum(sm, jnp.where(lane < CHUNK - s, pltpu.roll(sm, CHUNK - s, axis=1), -jnp.inf))
        s *= 2
    cm = jnp.where(fwd_row, pm, sm)
    return b, r, cm, wk, f, ml


def _mlstm_kernel(q_ref, k_ref, v_ref, g_ref, cwq_ref, cwk_ref, cbq_ref, cbk_ref, gh_ref,
                  out_ref, qc_ref, kt_ref, gate_ref, caug_ref, m_ref):
    NC = v_ref.shape[0]
    hps = HEADS_PER_STEP
    k_scale = HEAD_DIM ** -0.5

    cwq, cwk = cwq_ref[...], cwk_ref[...]
    cbq, cbk = cbq_ref[...], cbk_ref[...]
    n_sub = CHUNK // CONV_ROWS
    for c in range(NC):
        qb = jnp.concatenate([_conv_silu(q_ref, cwq, cbq, c * CHUNK + i * CONV_ROWS, CONV_ROWS)
                              for i in range(n_sub)], axis=0)
        kb = jnp.concatenate([_conv_silu(k_ref, cwk, cbk, c * CHUNK + i * CONV_ROWS, CONV_ROWS)
                              for i in range(n_sub)], axis=0) * k_scale
        for hd in range(hps):
            cols = slice(hd * HEAD_DIM, (hd + 1) * HEAD_DIM)
            qc_ref[hd, c] = qb[:, cols].astype(BF16)
            kt_ref[hd, c] = kb[:, cols].T

    t_idx = lax.broadcasted_iota(jnp.int32, (CHUNK, CHUNK), 0)
    s_idx = lax.broadcasted_iota(jnp.int32, (CHUNK, CHUNK), 1)
    ones = jnp.ones((CHUNK, HEAD_DIM), BF16)
    gh = gh_ref[...]

    for hd in range(hps):
        rows = _gate_rows(g_ref[hd].reshape(NC * 8, CHUNK), t_idx, s_idx)
        for kind, val in enumerate(rows):
            gate_ref[kind, hd] = val.reshape(NC, 8, CHUNK)

    def col_form(row):
        return jnp.broadcast_to(row, (CHUNK, CHUNK)).T

    def chunk_step(hd, c, direction):
        st = 2 * hd + direction
        row = slice(2 * direction, 2 * direction + 1)
        b_r = gate_ref[G_B, hd, c][row]
        r_r = gate_ref[G_R, hd, c][row]
        cm_r = gate_ref[G_CM, hd, c][row]
        wk_r = gate_ref[G_WK, hd, c][row]
        f_tot = gate_ref[G_F, hd, c][row]
        m_loc = gate_ref[G_ML, hd, c][row]
        causal = (s_idx <= t_idx) if direction == 0 else (s_idx >= t_idx)

        m_prev = m_ref[st][0:1]
        caug = caug_ref[st]
        q = qc_ref[hd, c]
        kt = kt_ref[hd, c]
        vaug = jnp.concatenate([v_ref[c, :, hd * HEAD_DIM:(hd + 1) * HEAD_DIM], ones], axis=1)

        b_c = col_form(b_r)
        mx_c = jnp.maximum(col_form(cm_r), m_prev)
        w_mat = jnp.where(causal, jnp.exp(r_r - mx_c), 0.0)
        s_inter = jnp.exp(m_prev - mx_c)
        scores = _dot(q, kt.astype(BF16))
        wqk = (w_mat * scores).astype(BF16)
        intra = _dot(wqk, vaug)
        carried = _dot(q, caug.astype(BF16))
        num = intra[:, 0:HEAD_DIM] + s_inter * carried[:, 0:HEAD_DIM]
        den = intra[:, HEAD_DIM:2 * HEAD_DIM] + s_inter * carried[:, HEAD_DIM:2 * HEAD_DIM]
        h = num / jnp.maximum(jnp.abs(den), jnp.exp(-(b_c + mx_c)))

        kw = (kt * wk_r).astype(BF16)
        c_loc = _dot(kw, vaug)
        m_new = jnp.maximum(f_tot + m_prev, m_loc)
        s_old = jnp.exp(f_tot + m_prev - m_new)
        s_new = jnp.exp(m_loc - m_new)
        wide = lambda a: jnp.concatenate([a, a], axis=1)
        caug_ref[st] = wide(s_old) * caug + wide(s_new) * c_loc
        m_ref[st] = jnp.broadcast_to(m_new, m_ref.shape[1:])
        return h

    caug_ref[...] = jnp.zeros_like(caug_ref)
    m_ref[...] = jnp.full(m_ref.shape, NEG_INIT, F32)

    def finish(hd, hs):
        cols = slice(hd * HEAD_DIM, (hd + 1) * HEAD_DIM)
        return hs * lax.rsqrt(jnp.mean(hs * hs, axis=-1, keepdims=True) + RMS_EPS) * gh[:, cols]

    def make_body(final):
        def body(i, carry):
            cf = i
            cb = NC - 1 - i
            for hd in range(hps):
                cols = slice(hd * HEAD_DIM, (hd + 1) * HEAD_DIM)
                h_f = chunk_step(hd, cf, 0)
                h_b = chunk_step(hd, cb, 1)
                if final:
                    out_ref[cf, :, cols] = finish(hd, out_ref[cf, :, cols] + h_f)
                    out_ref[cb, :, cols] = finish(hd, out_ref[cb, :, cols] + h_b)
                else:
                    out_ref[cf, :, cols] = h_f
                    out_ref[cb, :, cols] = h_b
            return carry
        return body

    lax.fori_loop(0, NC // 2, make_body(False), 0, unroll=MLSTM_UNROLL)
    lax.fori_loop(NC // 2, NC, make_body(True), 0, unroll=MLSTM_UNROLL)


def _mlstm(qk, v, gt, conv_w, conv_b, g_head):
    B, S = qk.shape[0], qk.shape[1]
    NC = S // CHUNK
    assert NC % 2 == 0
    hps = HEADS_PER_STEP
    w = hps * HEAD_DIM
    k_off = N_HEADS // hps
    seq = lambda off: pl.BlockSpec((None, S, w), lambda b, p: (b, 0, p + off))
    blk = lambda off: pl.BlockSpec((None, NC, CHUNK, w), lambda b, p: (b, 0, 0, p + off))
    par = lambda r, off: pl.BlockSpec((r, w), lambda b, p: (0, p + off))
    return pl.pallas_call(
        _mlstm_kernel,
        grid=(B, N_HEADS // hps),
        in_specs=[
            seq(0), seq(k_off), blk(0),
            pl.BlockSpec((None, hps, NC, 8, CHUNK), lambda b, p: (b, p, 0, 0, 0)),
            par(CONV_WIDTH, 0), par(CONV_WIDTH, k_off), par(1, 0), par(1, k_off), par(1, 0),
        ],
        out_specs=blk(0),
        out_shape=jax.ShapeDtypeStruct(v.shape, F32),
        scratch_shapes=[
            pltpu.VMEM((hps, NC, CHUNK, HEAD_DIM), BF16),
            pltpu.VMEM((hps, NC, HEAD_DIM, CHUNK), F32),
            pltpu.VMEM((6, hps, NC, 8, CHUNK), F32),
            pltpu.VMEM((2 * hps, HEAD_DIM, 2 * HEAD_DIM), F32),
            pltpu.VMEM((2 * hps, 8, LANES), F32),
        ],
        compiler_params=pltpu.CompilerParams(
            dimension_semantics=("parallel", "parallel"), vmem_limit_bytes=VMEM_LIMIT),
        name="mlstm",
    )(qk, qk, v, gt, conv_w, conv_w, conv_b, conv_b, g_head)


def _kv_kernel(mem_ref, g_ref, wkt_ref, wv_ref, kt_ref, v_ref):
    mn = _rms(mem_ref[...], g_ref[...]).astype(BF16)
    for h in range(N_XHEADS):
        kt_ref[h] = _dot_nt(wkt_ref[h], mn).astype(BF16)
        v_ref[h] = _dot(mn, wv_ref[h]).astype(BF16)


def _kv_proj(mem, g_mem, w_kt, w_v):
    B, M, D = mem.shape
    dh = D // N_XHEADS
    return pl.pallas_call(
        _kv_kernel,
        grid=(B,),
        in_specs=[
            pl.BlockSpec((None, M, D), lambda b: (b, 0, 0)),
            pl.BlockSpec((1, D), lambda b: (0, 0)),
            pl.BlockSpec(w_kt.shape, lambda b: (0, 0, 0)),
            pl.BlockSpec(w_v.shape, lambda b: (0, 0, 0)),
        ],
        out_specs=[
            pl.BlockSpec((None, N_XHEADS, dh, M), lambda b: (b, 0, 0, 0)),
            pl.BlockSpec((None, N_XHEADS, M, dh), lambda b: (b, 0, 0, 0)),
        ],
        out_shape=[
            jax.ShapeDtypeStruct((B, N_XHEADS, dh, M), BF16),
            jax.ShapeDtypeStruct((B, N_XHEADS, M, dh), BF16),
        ],
        compiler_params=pltpu.CompilerParams(
            dimension_semantics=("parallel",), vmem_limit_bytes=VMEM_LIMIT),
        name="kv_proj",
    )(mem, g_mem, w_kt, w_v)


def _mix_rows(r0, n, x_ref, hm_ref, o_ref, gu_ref, gv_ref, lng_ref, lnb_ref, ws_ref, bs_ref, wout_ref,
              gx_ref, wq_ref, kt_ref, vm_ref, wo_ref, gmoe_ref, wr_ref, br_ref, x2_ref, hn_ref):
    rs = slice(r0, r0 + n)
    gd = HEAD_DIM
    y_mlstm = _sigmoid(o_ref[rs, :]) * hm_ref[rs, :]
    gu = _gelu_tanh(gu_ref[rs, :])
    gv = _gelu_tanh(gv_ref[rs, :])
    mu = jnp.mean(gv, axis=-1, keepdims=True)
    gc = gv - mu
    gvn = gc * lax.rsqrt(jnp.mean(gc * gc, axis=-1, keepdims=True) + LN_EPS) * lng_ref[...] + lnb_ref[...]
    gvb = gvn.astype(BF16)
    rows = []
    for j in range(n // CHUNK):
        cols = []
        for g in range(N_GROUPS):
            sp = _dot(ws_ref[g], gvb[j * CHUNK:(j + 1) * CHUNK, g * gd:(g + 1) * gd]) + bs_ref[g]
            cols.append(sp)
        rows.append(jnp.concatenate(cols, axis=1))
    y_gmlp = gu * jnp.concatenate(rows, axis=0)
    n_ml = hm_ref.shape[1]
    mix = _dot(y_mlstm.astype(BF16), wout_ref[0:n_ml, :]) + _dot(y_gmlp.astype(BF16), wout_ref[n_ml:, :])
    x1 = x_ref[rs, :] + mix

    hq = _rms(x1, gx_ref[...]).astype(BF16)
    q = _dot(hq, wq_ref[...])
    dh = kt_ref.shape[1]
    scale = dh ** -0.5
    heads = []
    for h in range(N_XHEADS):
        s = _dot(q[:, h * dh:(h + 1) * dh].astype(BF16), kt_ref[h]) * scale
        e = jnp.exp(s - jnp.max(s, axis=-1, keepdims=True))
        p = e / jnp.sum(e, axis=-1, keepdims=True)
        heads.append(_dot(p.astype(BF16), vm_ref[h]))
    att = jnp.concatenate(heads, axis=1).astype(BF16)
    x2 = x1 + _dot(att, wo_ref[...])
    x2_ref[rs, :] = x2

    hn = _rms(x2, gmoe_ref[...])
    _store_row_tiles(hn_ref, r0, hn)
    return _dot_nt(wr_ref[...], hn.astype(BF16)) + br_ref[...]


def _mix_attn_kernel(x_ref, hm_ref, o_ref, gu_ref, gv_ref, lng_ref, lnb_ref, ws_ref, bs_ref, wout_ref,
                     gx_ref, wq_ref, kt_ref, vm_ref, wo_ref, gmoe_ref, wr_ref, br_ref, utri_ref,
                     x2_ref, hn_ref, rt_ref, cnt_ref, run_ref):
    tm = x_ref.shape[0]
    parts = [_mix_rows(r0, MIX_SUB, x_ref, hm_ref, o_ref, gu_ref, gv_ref, lng_ref, lnb_ref, ws_ref, bs_ref,
                       wout_ref, gx_ref, wq_ref, kt_ref, vm_ref, wo_ref, gmoe_ref, wr_ref, br_ref,
                       x2_ref, hn_ref)
             for r0 in range(0, tm, MIX_SUB)]
    n_log = N_EXPERT_GROUPS + N_EXPERTS + 4
    lg = jnp.concatenate(parts, axis=1)[0:n_log]

    rowi = lax.broadcasted_iota(jnp.int32, lg.shape, 0).astype(F32)
    none = float(LANES)
    gmask = rowi < N_EXPERT_GROUPS
    gmax = jnp.max(jnp.where(gmask, lg, -jnp.inf), axis=0, keepdims=True)
    ge = jnp.where(gmask, jnp.exp(lg - gmax), 0.0)
    p = ge / jnp.sum(ge, axis=0, keepdims=True)
    p_top = jnp.max(p, axis=0, keepdims=True)
    g_idx = jnp.min(jnp.where(jnp.where(gmask, p, -1.0) == p_top, rowi, none), axis=0, keepdims=True)
    lo = N_EXPERT_GROUPS + EXPERTS_PER_GROUP * g_idx
    el = jnp.where(rowi >= lo, jnp.where(rowi < lo + EXPERTS_PER_GROUP, lg, -jnp.inf), -jnp.inf)
    v1 = jnp.max(el, axis=0, keepdims=True)
    i1 = jnp.min(jnp.where(el == v1, rowi, none), axis=0, keepdims=True)
    el2 = jnp.where(rowi == i1, -jnp.inf, el)
    v2 = jnp.max(el2, axis=0, keepdims=True)
    i2 = jnp.min(jnp.where(el2 == v2, rowi, none), axis=0, keepdims=True)
    ex2 = jnp.exp(v2 - v1)
    den = 1.0 + ex2
    w1 = p_top * (1.0 / den)
    w2 = p_top * (ex2 / den)
    e1 = i1 - N_EXPERT_GROUPS
    e2 = i2 - N_EXPERT_GROUPS

    @pl.when(pl.program_id(0) == 0)
    def _():
        run_ref[...] = jnp.zeros_like(run_ref)

    erow = lax.broadcasted_iota(jnp.int32, (N_EXPERTS, tm), 0).astype(F32)
    is1 = erow == e1
    is2 = erow == e2
    oh = jnp.where(is1, 1.0, jnp.where(is2, 1.0, 0.0))
    run = run_ref[...]
    before = _dot(oh.astype(BF16), utri_ref[...]) + jnp.concatenate([run] * (tm // LANES), axis=1)
    rank1 = jnp.sum(jnp.where(is1, before, 0.0), axis=0, keepdims=True)
    rank2 = jnp.sum(jnp.where(is2, before, 0.0), axis=0, keepdims=True)
    total = run + jnp.sum(oh, axis=1, keepdims=True)
    run_ref[...] = total
    cnt_ref[...] = total
    zero = jnp.zeros_like(w1)
    rt_ref[...] = jnp.concatenate([w1, w2, e1, e2, rank1, rank2, zero, zero], axis=0)


def _mix_attn(x, hm, o, gu, gv, ln_g, ln_b, w_s, b_s, w_out, g_x, w_q, kt, vm, w_o, g_moe, w_r, b_r, S):
    T, D = x.shape
    tm = MIX_TILE
    per_b = S // tm
    row = lambda i: (i, 0)
    c2 = lambda i: (0, 0)
    c3 = lambda i: (0, 0, 0)
    full = lambda a: pl.BlockSpec(a.shape, c2 if a.ndim == 2 else c3)
    tile = lambda a: pl.BlockSpec((tm, a.shape[1]), row)
    ti = lax.broadcasted_iota(jnp.int32, (tm, tm), 0)
    si = lax.broadcasted_iota(jnp.int32, (tm, tm), 1)
    utri = jnp.where(ti < si, 1.0, 0.0).astype(BF16)
    return pl.pallas_call(
        _mix_attn_kernel,
        grid=(T // tm,),
        in_specs=[
            tile(x), tile(hm), tile(o), tile(gu), tile(gv),
            full(ln_g), full(ln_b), full(w_s), full(b_s), full(w_out),
            full(g_x), full(w_q),
            pl.BlockSpec((None,) + kt.shape[1:], lambda i: (i // per_b, 0, 0, 0)),
            pl.BlockSpec((None,) + vm.shape[1:], lambda i: (i // per_b, 0, 0, 0)),
            full(w_o), full(g_moe), full(w_r), full(b_r), full(utri),
        ],
        out_specs=[pl.BlockSpec((tm, D), row), pl.BlockSpec((tm * TILE_ROWS, LANES), row),
                   pl.BlockSpec((8, tm), lambda i: (0, i)),
                   pl.BlockSpec((N_EXPERTS, LANES), c2)],
        out_shape=[
            jax.ShapeDtypeStruct((T, D), F32),
            jax.ShapeDtypeStruct((T * TILE_ROWS, LANES), F32),
            jax.ShapeDtypeStruct((8, T), F32),
            jax.ShapeDtypeStruct((N_EXPERTS, LANES), F32),
        ],
        scratch_shapes=[pltpu.VMEM((N_EXPERTS, LANES), F32)],
        compiler_params=pltpu.CompilerParams(
            dimension_semantics=("arbitrary",), vmem_limit_bytes=VMEM_LIMIT),
        name="mix_attn",
    )(x, hm, o, gu, gv, ln_g, ln_b, w_s, b_s, w_out, g_x, w_q, kt, vm, w_o, g_moe, w_r, b_r, utri)


def _idx_copy(idx_hbm, block, idx_smem, slot, isem):
    return pltpu.make_async_copy(idx_hbm.at[block], idx_smem.at[slot], isem.at[slot])


def _dispatch_kernel(pad_start_ref, pad_n_ref, dd_hbm, h_ref, xd_hbm, idx, zrow, sem, isem, zsem):
    i = pl.program_id(0)
    nb = pl.num_programs(0)
    R = h_ref.shape[0] // TILE_ROWS
    slot = lax.rem(i, 2)

    def tile_at(ref, first):
        return ref.at[pl.ds(pl.multiple_of(first, TILE_ROWS), TILE_ROWS), :]

    def row_copy(r, dst8):
        return pltpu.make_async_copy(tile_at(h_ref, r * TILE_ROWS), tile_at(xd_hbm, dst8), sem)

    def zero_copy(dst):
        return pltpu.make_async_copy(zrow, tile_at(xd_hbm, dst * TILE_ROWS), zsem)

    @pl.when(i == 0)
    def _():
        first = _idx_copy(dd_hbm, 0, idx, 0, isem)
        first.start()
        zrow[...] = jnp.zeros_like(zrow)
        for e in range(pad_start_ref.shape[0]):
            base = pad_start_ref[e]
            count = pad_n_ref[e]

            def start_body(j, carry, base=base):
                zero_copy(base + j).start()
                return carry

            def wait_body(j, carry):
                zero_copy(0).wait()
                return carry

            lax.fori_loop(0, count, start_body, 0)
            lax.fori_loop(0, count, wait_body, 0)
        first.wait()

    @pl.when(i + 1 < nb)
    def _():
        _idx_copy(dd_hbm, i + 1, idx, 1 - slot, isem).start()

    for s in range(2):
        @pl.when(slot == s)
        def _(s=s):
            for r in range(R):
                for k in range(TOP_K):
                    row_copy(r, idx[s, k * R + r]).start(priority=k)
    for r in range(TOP_K * R):
        row_copy(0, 0).wait()

    @pl.when(i + 1 < nb)
    def _():
        _idx_copy(dd_hbm, i + 1, idx, 1 - slot, isem).wait()


def _dispatch(pad_start, pad_n, dd8, hn, n_slots):
    nt, two_r = dd8.shape
    R = two_r // TOP_K
    any_spec = pl.BlockSpec(memory_space=pl.ANY)
    return pl.pallas_call(
        _dispatch_kernel,
        grid_spec=pltpu.PrefetchScalarGridSpec(
            num_scalar_prefetch=2,
            grid=(nt,),
            in_specs=[any_spec, pl.BlockSpec((R * TILE_ROWS, LANES), lambda i, ps, pn: (i, 0))],
            out_specs=any_spec,
            scratch_shapes=[pltpu.SMEM((2, two_r), jnp.int32), pltpu.VMEM((TILE_ROWS, LANES), hn.dtype),
                            pltpu.SemaphoreType.DMA(()), pltpu.SemaphoreType.DMA((2,)),
                            pltpu.SemaphoreType.DMA(())],
        ),
        out_shape=jax.ShapeDtypeStruct((n_slots * TILE_ROWS, LANES), hn.dtype),
        compiler_params=pltpu.CompilerParams(
            dimension_semantics=("arbitrary",), vmem_limit_bytes=VMEM_LIMIT),
        name="dispatch",
    )(pad_start, pad_n, dd8, hn)


def _expert_kernel(blk_e_ref, n_used_ref, x_ref, wg_ref, wu_ref, wd_ref, y_ref):
    used = pl.program_id(0) < n_used_ref[0]

    @pl.when(used)
    def _():
        R = x_ref.shape[0] // TILE_ROWS
        xb = _load_row_tiles(x_ref, 0, R).astype(BF16)
        act = (_silu(_dot(xb, wg_ref[...])) * _dot(xb, wu_ref[...])).astype(BF16)
        _store_row_tiles(y_ref, 0, _dot(act, wd_ref[...]))

    @pl.when(jnp.logical_not(used))
    def _():
        y_ref[...] = jnp.zeros_like(y_ref)


def _experts(blk_e, n_used, x_disp, w_g, w_u, w_d):
    R = EXPERT_BLOCK
    D = w_g.shape[1]
    P = x_disp.shape[0] // TILE_ROWS
    nb = P // R
    de = w_d.shape[1]
    tiles = (R * TILE_ROWS, LANES)
    last = lambda i, nu: jnp.minimum(i, nu[0] - 1)
    return pl.pallas_call(
        _expert_kernel,
        grid_spec=pltpu.PrefetchScalarGridSpec(
            num_scalar_prefetch=2,
            grid=(nb,),
            in_specs=[
                pl.BlockSpec(tiles, lambda i, be, nu: (last(i, nu), 0)),
                pl.BlockSpec((None, D, de), lambda i, be, nu: (be[last(i, nu)], 0, 0)),
                pl.BlockSpec((None, D, de), lambda i, be, nu: (be[last(i, nu)], 0, 0)),
                pl.BlockSpec((None, de, D), lambda i, be, nu: (be[last(i, nu)], 0, 0)),
            ],
            out_specs=pl.BlockSpec(tiles, lambda i, be, nu: (i, 0)),
        ),
        out_shape=jax.ShapeDtypeStruct((P * TILE_ROWS, LANES), F32),
        compiler_params=pltpu.CompilerParams(
            dimension_semantics=("arbitrary",), vmem_limit_bytes=VMEM_LIMIT),
        name="experts",
    )(blk_e, n_used, x_disp, w_g, w_u, w_d)


def _combine_kernel(dd_hbm, y_hbm, x2_ref, rt_ref, gf_ref, out_ref, buf, idx, sem, isem):
    i = pl.program_id(0)
    nb = pl.num_programs(0)
    R = x2_ref.shape[0]
    slot = lax.rem(i, 2)

    def row_copy(src8, s, r):
        return pltpu.make_async_copy(y_hbm.at[pl.ds(pl.multiple_of(src8, TILE_ROWS), TILE_ROWS), :],
                                     buf.at[s, pl.ds(r * TILE_ROWS, TILE_ROWS), :], sem.at[s])

    def issue_rows(s):
        for r in range(TOP_K * R):
            row_copy(idx[s, r], s, r).start(priority=r % 2)

    @pl.when(i == 0)
    def _():
        first = _idx_copy(dd_hbm, 0, idx, 0, isem)
        first.start()
        first.wait()
        issue_rows(0)

        @pl.when(nb > 1)
        def _():
            _idx_copy(dd_hbm, 1, idx, 1, isem).start()

    for s in range(2):
        @pl.when((i + 1 < nb) & (slot == 1 - s))
        def _(s=s):
            _idx_copy(dd_hbm, i + 1, idx, s, isem).wait()
            issue_rows(s)

    @pl.when(i + 2 < nb)
    def _():
        _idx_copy(dd_hbm, i + 2, idx, slot, isem).start()

    for r in range(TOP_K * R):
        row_copy(0, slot, r).wait()

    rt = rt_ref[...]
    D = x2_ref.shape[1]
    for j in range(R // LANES):
        rows = slice(j * LANES, (j + 1) * LANES)
        wa = jnp.broadcast_to(rt[0:1, rows], (LANES, LANES)).T
        wb = jnp.broadcast_to(rt[1:2, rows], (LANES, LANES)).T
        wa = jnp.concatenate([wa] * (D // LANES), axis=1)
        wb = jnp.concatenate([wb] * (D // LANES), axis=1)
        ya = _load_row_tiles(buf, j * LANES, LANES, lead=(slot,))
        yb = _load_row_tiles(buf, R + j * LANES, LANES, lead=(slot,))
        out_ref[rows, :] = _rms(x2_ref[rows, :] + (wa * ya + wb * yb), gf_ref[...])


def _combine(dd, y_disp, x2, route, g_final):
    T, D = x2.shape
    R = COMBINE_TILE
    row = lambda i: (i, 0)
    return pl.pallas_call(
        _combine_kernel,
        grid=(T // R,),
        in_specs=[
            pl.BlockSpec(memory_space=pl.ANY),
            pl.BlockSpec(memory_space=pl.ANY),
            pl.BlockSpec((R, D), row),
            pl.BlockSpec((8, R), lambda i: (0, i)),
            pl.BlockSpec((1, D), lambda i: (0, 0)),
        ],
        out_specs=pl.BlockSpec((R, D), row),
        scratch_shapes=[pltpu.VMEM((2, TOP_K * R * TILE_ROWS, LANES), F32), pltpu.SMEM((2, TOP_K * R), jnp.int32),
                        pltpu.SemaphoreType.DMA((2,)), pltpu.SemaphoreType.DMA((2,))],
        out_shape=jax.ShapeDtypeStruct((T, D), F32),
        compiler_params=pltpu.CompilerParams(
            dimension_semantics=("arbitrary",), vmem_limit_bytes=VMEM_LIMIT),
        name="combine",
    )(dd, y_disp, x2, route, g_final)


def _route_tables(route, counts_row, T):
    R = EXPERT_BLOCK
    counts = counts_row[:, 0].astype(jnp.int32)
    padded = ((counts + R - 1) // R) * R
    pend = jnp.cumsum(padded)
    pstart = pend - padded
    nb = (T * TOP_K) // R + N_EXPERTS
    first_slot = jnp.arange(nb, dtype=jnp.int32) * R
    blk_e = jnp.sum((pend[None, :] <= first_slot[:, None]).astype(jnp.int32), axis=1)
    blk_e = jnp.minimum(blk_e, N_EXPERTS - 1)
    n_used = (pend[N_EXPERTS - 1:N_EXPERTS] // R).astype(jnp.int32)
    e = route[2:4].astype(jnp.int32)
    rank = route[4:6].astype(jnp.int32)
    sel = e[None, :, :] == jnp.arange(N_EXPERTS, dtype=jnp.int32)[:, None, None]
    dest = jnp.sum(jnp.where(sel, pstart[:, None, None], 0), axis=0) + rank
    d = (dest * TILE_ROWS).reshape(TOP_K, T // COMBINE_TILE, COMBINE_TILE)
    dd = jnp.transpose(d, (1, 0, 2)).reshape(T // COMBINE_TILE, TOP_K * COMBINE_TILE)
    total = pend[N_EXPERTS - 1:N_EXPERTS]
    pad_start = jnp.concatenate([pstart + counts, total])
    pad_n = jnp.concatenate([padded - counts, nb * R - total])
    return blk_e, n_used, dd, nb * R, pad_start, pad_n


def kernel(x, mem, g_mix, w_in, conv_w, conv_b, gate_b, g_head, ln_v_g, ln_v_b, w_s, b_s, w_out,
           g_xattn, g_mem, w_q_x, w_kv_x, w_o_x, g_moe, w_rg, b_rg, w_re, b_re,
           w_gate, w_up, w_down, g_final):
    B, S, D = x.shape
    T = B * S
    NC = S // CHUNK
    d_ml = N_HEADS * HEAD_DIM
    n_gates = 4 * N_HEADS
    assert w_in.shape[0] == 1, "one layer"
    assert S % PROJ_TILE == 0 and S % MIX_TILE == 0 and T % COMBINE_TILE == 0
    assert (T * TOP_K) % EXPERT_BLOCK == 0

    w = w_in[0]
    c_g = 4 * d_ml
    w_main = jnp.concatenate([w[:, 0:c_g], w[:, c_g + n_gates:]], axis=1).astype(BF16)
    wg = w[:, c_g:c_g + n_gates].reshape(D, 4, N_HEADS)
    wg = jnp.pad(jnp.transpose(wg, (2, 1, 0)), ((0, 0), (0, 4), (0, 0)))
    w_gt = wg.reshape(N_HEADS * 8, D).astype(BF16)
    gb = jnp.pad(jnp.transpose(gate_b[0].reshape(4, N_HEADS), (1, 0)), ((0, 0), (0, 4)))
    gb = gb.reshape(N_HEADS * 8, 1).astype(F32)

    qk, v, o, gu, gv, gt = _proj_in(x, g_mix[0:1], w_main, w_gt, gb)

    h_ml = _mlstm(qk.reshape(B, S, 2 * d_ml), v.reshape(B, NC, CHUNK, d_ml), gt,
                  conv_w[0].reshape(CONV_WIDTH, 2 * d_ml), conv_b[0:1], g_head[0:1])
    h_ml = h_ml.reshape(T, d_ml)

    dh = D // N_XHEADS
    w_kv = w_kv_x[0]
    w_kt = jnp.transpose(w_kv[:, 0:D].reshape(D, N_XHEADS, dh), (1, 2, 0)).astype(BF16)
    w_v = jnp.transpose(w_kv[:, D:2 * D].reshape(D, N_XHEADS, dh), (1, 0, 2)).astype(BF16)
    kt, vm = _kv_proj(mem, g_mem[0:1], w_kt, w_v)

    w_r = jnp.concatenate([w_rg[0], w_re[0]], axis=1).T
    w_r = jnp.pad(w_r, ((0, LANES - w_r.shape[0]), (0, 0))).astype(BF16)
    b_r = jnp.concatenate([b_rg[0], b_re[0]])
    b_r = jnp.pad(b_r, (0, LANES - b_r.shape[0])).reshape(LANES, 1).astype(F32)
    bs_b = jnp.broadcast_to(b_s[0][:, :, None], (N_GROUPS, CHUNK, HEAD_DIM)).astype(F32)

    x2, hn, route, counts = _mix_attn(
        x.reshape(T, D), h_ml, o, gu, gv, ln_v_g[0:1], ln_v_b[0:1], w_s[0].astype(BF16), bs_b,
        w_out[0].astype(BF16), g_xattn[0:1], w_q_x[0].astype(BF16), kt, vm, w_o_x[0].astype(BF16),
        g_moe[0:1], w_r, b_r, S)

    blk_e, n_used, dd, n_slots, pad_start, pad_n = _route_tables(route, counts, T)
    x_disp = _dispatch(pad_start, pad_n, dd, hn, n_slots)

    y_disp = _experts(blk_e, n_used, x_disp, w_gate[0].astype(BF16), w_up[0].astype(BF16),
                      w_down[0].astype(BF16))

    out = _combine(dd, y_disp, x2, route, g_final.reshape(1, D))
    return out.reshape(B, S, D)
```
